```python
import math
import jax
import jax.numpy as jnp
from jax import lax
import numpy as np

D_MODEL = 1024
BATCH = 16
SEQ = 2048
DEPTH = 4

GRID_W = 64
CTX_LEN = 256

SSD_HEADS = 16
SSD_HEADDIM = 64
SSD_D_INNER = SSD_HEADS * SSD_HEADDIM
SSD_GROUPS = 4
SSD_STATE = 128
SSD_CONV = 5
SSD_CHUNK = 128
SSD_XBC = SSD_D_INNER + 2 * SSD_GROUPS * SSD_STATE

NA_HEADS = 16
NA_HEAD_DIM = 64
NA_WIDTH = NA_HEADS * NA_HEAD_DIM
NA_WIN_ROWS = 8
NA_WIN_COLS = 16

POOL_WINDOWS = (2, 4, 8, 16)
POOL_GROUP = 192
POOL_WIDTH = 4 * POOL_GROUP

FNET_HEADS = 4
FNET_HEAD_DIM = 192
FNET_WIDTH = FNET_HEADS * FNET_HEAD_DIM

N_BRANCH = 4
IN_WIDTHS = (SSD_D_INNER, SSD_XBC, 2 * SSD_HEADS, 3 * NA_WIDTH, POOL_WIDTH, FNET_WIDTH, N_BRANCH * D_MODEL)
IN_WIDTH = SSD_D_INNER + SSD_XBC + 2 * SSD_HEADS + 3 * NA_WIDTH + POOL_WIDTH + FNET_WIDTH + N_BRANCH * D_MODEL

D_FF = 4 * D_MODEL
ROPE_THETA = 10000.0
DEEPNORM_ALPHA = (2 * DEPTH) ** 0.25
DEEPNORM_BETA = (8 * DEPTH) ** -0.25
LN_EPS = 1e-6
RMS_EPS = 1e-5

kernel_name = 'hybrid_ssd_natten_pool_fourier_dit'

F32 = jnp.float32


def layer_norm(x, gain=None, bias=None):
    xf = x.astype(F32)
    mu = jnp.mean(xf, -1, keepdims=True)
    var = jnp.mean(jnp.square(xf - mu), -1, keepdims=True)
    y = (xf - mu) * lax.rsqrt(var + LN_EPS)
    if gain is not None:
        y = y * gain.astype(F32) + bias.astype(F32)
    return y.astype(x.dtype)


def modulate(h, shift, scale):
    return h * (1 + scale) + shift


def split_in_proj(p):
    idx = np.cumsum(np.array(IN_WIDTHS))[:-1].tolist()
    return jnp.split(p, idx, axis=-1)


def axial_rope(n, dim):
    pos = jnp.arange(n)
    quarter = dim // 4
    inv_freq = ROPE_THETA ** (-jnp.arange(quarter, dtype=F32) / quarter)
    row = (pos // GRID_W).astype(F32)[:, None] * inv_freq
    col = (pos % GRID_W).astype(F32)[:, None] * inv_freq
    ang = jnp.concatenate([row, col], -1)
    return jnp.cos(ang), jnp.sin(ang)


def apply_rope(t, cos, sin):
    half = t.shape[-1] // 2
    tf = t.astype(F32)
    t1, t2 = tf[..., :half], tf[..., half:]
    c, s = cos[None, :, None, :], sin[None, :, None, :]
    return jnp.concatenate([t1 * c - t2 * s, t1 * s + t2 * c], -1).astype(t.dtype)


def depthwise_conv_centred(x, w, b):
    k = w.shape[0]
    y = lax.conv_general_dilated(x, w.astype(x.dtype)[:, None, :], window_strides=(1,),
                                 padding=((k // 2, k // 2),), dimension_numbers=('NWC', 'WIO', 'NWC'),
                                 feature_group_count=x.shape[-1])
    return y + b


def grouped_rms_norm(y, w):
    bsz, n, d = y.shape
    yf = y.astype(F32).reshape(bsz, n, SSD_GROUPS, d // SSD_GROUPS)
    yf = yf * lax.rsqrt(jnp.mean(jnp.square(yf), -1, keepdims=True) + RMS_EPS)
    return (yf.reshape(bsz, n, d) * w.astype(F32)).astype(y.dtype)


def ssd_chunked(xs, dt, a, bm, cm, init_state):
    bsz, n, nh, hp = xs.shape
    ng, ns = bm.shape[2], bm.shape[3]
    nr = nh // ng
    t = SSD_CHUNK
    nc = n // t
    xdt = (xs.astype(F32) * dt[..., None]).reshape(bsz, nc, t, ng, nr, hp)
    a_cum = jnp.cumsum((dt * a.astype(F32)).reshape(bsz, nc, t, ng, nr), axis=2)
    bmc = bm.astype(F32).reshape(bsz, nc, t, ng, ns)
    cmc = cm.astype(F32).reshape(bsz, nc, t, ng, ns)
    lower = jnp.tril(jnp.ones((t, t), dtype=bool))[:, :, None, None]
    seg = a_cum[:, :, :, None] - a_cum[:, :, None, :]
    decay = jnp.exp(jnp.where(lower, seg, -jnp.inf))
    cb = jnp.einsum('bclgn,bcsgn->bclsg', cmc, bmc)
    y_diag = jnp.einsum('bclsg,bclsgr,bcsgrp->bclgrp', cb, decay, xdt)
    decay_to_end = jnp.exp(a_cum[:, :, -1:] - a_cum)
    chunk_states = jnp.einsum('bclgn,bclgr,bclgrp->bcgrpn', bmc, decay_to_end, xdt)
    chunk_decay = jnp.exp(a_cum[:, :, -1])

    def step(state, inp):
        s_c, d_c = inp
        return state * d_c[..., None, None] + s_c, state

    final, states_in = lax.scan(step, init_state,
                                (jnp.moveaxis(chunk_states, 1, 0), jnp.moveaxis(chunk_decay, 1, 0)))
    states_in = jnp.moveaxis(states_in, 0, 1)
    y_off = jnp.einsum('bclgn,bcgrpn,bclgr->bclgrp', cmc, states_in, jnp.exp(a_cum))
    return (y_diag + y_off).reshape(bsz, n, nh, hp), final


def ssd_bidirectional(xs, dt, bm, cm, a, init_fwd, init_bwd):
    rev = lambda t: jnp.flip(t, axis=1)
    y_f, s_f = ssd_chunked(xs, dt[:, :, 0], a[0], bm, cm, init_fwd)
    y_b, s_b = ssd_chunked(rev(xs), rev(dt[:, :, 1]), a[1], rev(bm), rev(cm), init_bwd)
    return (y_f + rev(y_b)).astype(xs.dtype), s_f, s_b


def ssd_prepare(xbc, dt_raw, lp, use_rope):
    xbc = jax.nn.silu(depthwise_conv_centred(xbc, lp['ssd_conv_w'], lp['ssd_conv_b']))
    xs, bm, cm = jnp.split(xbc, [SSD_D_INNER, SSD_D_INNER + SSD_GROUPS * SSD_STATE], axis=-1)
    bsz, n = xs.shape[0], xs.shape[1]
    xs = xs.reshape(bsz, n, SSD_HEADS, SSD_HEADDIM)
    bm = bm.reshape(bsz, n, SSD_GROUPS, SSD_STATE)
    cm = cm.reshape(bsz, n, SSD_GROUPS, SSD_STATE)
    if use_rope:
        cos, sin = axial_rope(n, SSD_STATE)
        bm = apply_rope(bm, cos, sin)
        cm = apply_rope(cm, cos, sin)
    dt = jax.nn.softplus(dt_raw.astype(F32).reshape(bsz, n, 2, SSD_HEADS) + lp['ssd_dt_bias'].astype(F32))
    return xs, dt, bm, cm


def ssd_output(y, xs, z, lp):
    bsz, n = xs.shape[0], xs.shape[1]
    y = (y + lp['ssd_d'][:, None] * xs).reshape(bsz, n, SSD_D_INNER)
    y = grouped_rms_norm(y * jax.nn.silu(z), lp['ssd_norm_w'])
    return y @ lp['ssd_out_w']


def split_qkv(qkv):
    bsz, n = qkv.shape[0], qkv.shape[1]
    qkv = qkv.reshape(bsz, n, 3, NA_HEADS, NA_HEAD_DIM)
    return qkv[:, :, 0] * (NA_HEAD_DIM ** -0.5), qkv[:, :, 1], qkv[:, :, 2]


def neighbourhood_attention(q, k, v, k_ctx, v_ctx, rpb):
    bsz, n, nh, hd = q.shape
    rows = n // GRID_W
    kr = min(NA_WIN_ROWS, rows)
    kc = NA_WIN_COLS
    n_loc = kr * kc
    q_g = q.reshape(bsz, rows, GRID_W, nh, hd)
    k_g = k.reshape(bsz, rows, GRID_W, nh, hd)
    v_g = v.reshape(bsz, rows, GRID_W, nh, hd)
    row_start = jnp.clip(jnp.arange(rows) - kr // 2, 0, rows - kr)
    cols = jnp.arange(GRID_W)
    col_idx = jnp.clip(cols - kc // 2, 0, GRID_W - kc)[:, None] + jnp.arange(kc)[None, :]
    dc_idx = (col_idx - cols[:, None] + NA_WIN_COLS - 1)[None]

    def one_row(r):
        rs = row_start[r]
        q_r = lax.dynamic_index_in_dim(q_g, r, axis=1, keepdims=False)
        k_win = lax.dynamic_slice_in_dim(k_g, rs, kr, axis=1)[:, :, col_idx]
        v_win = lax.dynamic_slice_in_dim(v_g, rs, kr, axis=1)[:, :, col_idx]
        dr_idx = (rs + jnp.arange(kr) - r + NA_WIN_ROWS - 1)[:, None, None]
        bias = rpb[:, dr_idx, dc_idx].transpose(0, 2, 1, 3).astype(F32)
        s_loc = jnp.einsum('bjhd,bajchd->bhjac', q_r, k_win).astype(F32) + bias
        s_ctx = jnp.einsum('bjhd,bmhd->bhjm', q_r, k_ctx).astype(F32)
        p = jax.nn.softmax(jnp.concatenate([s_loc.reshape(bsz, nh, GRID_W, n_loc), s_ctx], -1), -1)
        p = p.astype(v.dtype)
        o = jnp.einsum('bhjac,bajchd->bjhd', p[..., :n_loc].reshape(bsz, nh, GRID_W, kr, kc), v_win)
        return o + jnp.einsum('bhjm,bmhd->bjhd', p[..., n_loc:], v_ctx)

    out = lax.map(one_row, jnp.arange(rows))
    return jnp.moveaxis(out, 0, 1).reshape(bsz, n, nh * hd)


def context_attention(q, k, v):
    s = jnp.einsum('bqhd,bkhd->bhqk', q, k).astype(F32)
    p = jax.nn.softmax(s, -1).astype(v.dtype)
    o = jnp.einsum('bhqk,bkhd->bqhd', p, v)
    return o.reshape(q.shape[0], q.shape[1], NA_WIDTH)


def pool_branch(u, lp):
    bsz, n, _ = u.shape
    uf = u.astype(F32)
    csum = jnp.pad(jnp.cumsum(uf, axis=1), ((0, 0), (1, 0), (0, 0)))
    pos = jnp.arange(n)
    parts = []
    for gi, win in enumerate(POOL_WINDOWS):
        sl = slice(gi * POOL_GROUP, (gi + 1) * POOL_GROUP)
        lo = jnp.clip(pos - win // 2, 0, n)
        hi = jnp.clip(pos + win - win // 2, 0, n)
        cnt = (hi - lo).astype(F32)[None, :, None]
        parts.append((csum[:, hi, sl] - csum[:, lo, sl]) / cnt - uf[:, :, sl])
    pooled = jnp.concatenate(parts, -1).astype(u.dtype).reshape(bsz, n, len(POOL_WINDOWS), POOL_GROUP)
    mapped = jnp.einsum('bngc,gcd->bngd', pooled, lp['pool_w']).reshape(bsz, n, POOL_WIDTH)
    return (mapped * lp['pool_scale']) @ lp['pool_out_w']


def fourier_branch(u, lp):
    bsz, n, _ = u.shape
    uf = u.astype(F32).reshape(bsz, n, FNET_HEADS, FNET_HEAD_DIM)
    mixed = jnp.fft.fft2(uf, axes=(1, 3), norm='ortho').real
    return mixed.reshape(bsz, n, FNET_WIDTH).astype(u.dtype) @ lp['fnet_out_w']


def merge_and_project(y_ssd, y_na, y_pool, y_fnet, gate_logits, w_out):
    g = jax.nn.sigmoid(gate_logits.astype(F32)).astype(y_ssd.dtype)
    g_ssd, g_na, g_pool, g_fnet = jnp.split(g, N_BRANCH, axis=-1)
    return (g_ssd * y_ssd + g_na * y_na + g_pool * y_pool + g_fnet * y_fnet) @ w_out


def token_mixing(h, hc, lp, need_ctx):
    z_x, xbc_x, dt_x, qkv_x, pool_x, fnet_x, gate_x = split_in_proj(h @ lp['in_w'] + lp['in_b'])
    z_c, xbc_c, dt_c, qkv_c, pool_c, fnet_c, gate_c = split_in_proj(hc @ lp['in_w'] + lp['in_b'])

    a = -jnp.exp(lp['ssd_a_log'].astype(F32))
    xs_c, dtc, bm_c, cm_c = ssd_prepare(xbc_c, dt_c, lp, False)
    xs_x, dtx, bm_x, cm_x = ssd_prepare(xbc_x, dt_x, lp, True)
    zero = jnp.zeros((hc.shape[0], SSD_GROUPS, SSD_HEADS // SSD_GROUPS, SSD_HEADDIM, SSD_STATE), F32)
    y_c, s_fwd, s_bwd = ssd_bidirectional(xs_c, dtc, bm_c, cm_c, a, zero, zero)
    y_x, _, _ = ssd_bidirectional(xs_x, dtx, bm_x, cm_x, a, s_fwd, s_bwd)
    ssd_lat = ssd_output(y_x, xs_x, z_x, lp)

    q_x, k_x, v_x = split_qkv(qkv_x)
    q_c, k_c, v_c = split_qkv(qkv_c)
    na_lat = neighbourhood_attention(q_x, k_x, v_x, k_c, v_c, lp['na_rpb']) @ lp['na_out_w']

    mix_x = merge_and_project(ssd_lat, na_lat, pool_branch(pool_x, lp), fourier_branch(fnet_x, lp),
                              gate_x, lp['mix_out_w'])
    mix_c = None
    if need_ctx:
        ssd_ctx = ssd_output(y_c, xs_c, z_c, lp)
        na_ctx = context_attention(q_c, k_c, v_c) @ lp['na_out_w']
        mix_c = merge_and_project(ssd_ctx, na_ctx, pool_branch(pool_c, lp), fourier_branch(fnet_c, lp),
                                  gate_c, lp['mix_out_w'])
    return mix_x, mix_c


def squared_relu_mlp(h, lp):
    a = jax.nn.relu(h @ lp['mlp_up_w'] + lp['mlp_up_b'])
    return jnp.square(a) @ lp['mlp_down_w'] + lp['mlp_down_b']


def setup_inputs(seed: int = 0) -> dict:
    key = jax.random.key(seed)
    ks = list(jax.random.split(key, 32))
    L = DEPTH

    def nrm(i, shape, scale):
        return jax.random.normal(ks[i], shape, F32) * scale

    dt0 = jnp.exp(jax.random.uniform(ks[10], (L, 2, SSD_HEADS), F32, math.log(1e-3), math.log(1e-1)))
    return {
        'x': nrm(0, (BATCH, SEQ, D_MODEL), 1.0),
        'c': nrm(1, (BATCH, D_MODEL), 1.0),
        'ctx': nrm(2, (BATCH, CTX_LEN, D_MODEL), 1.0),
        'c_ctx': nrm(3, (D_MODEL,), 1.0),
        'ada_w': nrm(4, (L, D_MODEL, 6 * D_MODEL), 0.5 * D_MODEL ** -0.5),
        'ada_b': nrm(5, (L, 6 * D_MODEL), 0.02),
        'in_w': nrm(6, (L, D_MODEL, IN_WIDTH), D_MODEL ** -0.5),
        'in_b': nrm(7, (L, IN_WIDTH), 0.02),
        'ssd_conv_w': nrm(8, (L, SSD_CONV, SSD_XBC), SSD_CONV ** -0.5),
        'ssd_conv_b': nrm(9, (L, SSD_XBC), 0.02),
        'ssd_dt_bias': dt0 + jnp.log(-jnp.expm1(-dt0)),
        'ssd_a_log': jnp.log(jax.random.uniform(ks[11], (L, 2, SSD_HEADS), F32, 1.0, 16.0)),
        'ssd_d': 1.0 + nrm(12, (L, SSD_HEADS), 0.1),
        'ssd_norm_w': 1.0 + nrm(13, (L, SSD_D_INNER), 0.02),
        'ssd_out_w': nrm(14, (L, SSD_D_INNER, D_MODEL), SSD_D_INNER ** -0.5),
        'na_rpb': nrm(15, (L, NA_HEADS, 2 * NA_WIN_ROWS - 1, 2 * NA_WIN_COLS - 1), 0.02),
        'na_out_w': nrm(16, (L, NA_WIDTH, D_MODEL), NA_WIDTH ** -0.5),
        'pool_w': nrm(17, (L, len(POOL_WINDOWS), POOL_GROUP, POOL_GROUP), POOL_GROUP ** -0.5),
        'pool_scale': 1.0 + nrm(18, (L, POOL_WIDTH), 0.02),
        'pool_out_w': nrm(19, (L, POOL_WIDTH, D_MODEL), POOL_WIDTH ** -0.5),
        'fnet_out_w': nrm(20, (L, FNET_WIDTH, D_MODEL), FNET_WIDTH ** -0.5),
        'mix_out_w': nrm(21, (L, D_MODEL, D_MODEL), DEEPNORM_BETA * D_MODEL ** -0.5),
        'ln1_g': 1.0 + nrm(22, (L, D_MODEL), 0.02),
        'ln1_b': nrm(23, (L, D_MODEL), 0.02),
        'mlp_up_w': nrm(24, (L, D_MODEL, D_FF), D_MODEL ** -0.5),
        'mlp_up_b': nrm(25, (L, D_FF), 0.02),
        'mlp_down_w': nrm(26, (L, D_FF, D_MODEL), DEEPNORM_BETA * D_FF ** -0.5),
        'mlp_down_b': nrm(27, (L, D_MODEL), 0.02),
        'ln2_g': 1.0 + nrm(28, (L, D_MODEL), 0.02),
        'ln2_b': nrm(29, (L, D_MODEL), 0.02),
    }


def reference(x, c, ctx, c_ctx, ada_w, ada_b, in_w, in_b, ssd_conv_w, ssd_conv_b, ssd_dt_bias, ssd_a_log,
              ssd_d, ssd_norm_w, ssd_out_w, na_rpb, na_out_w, pool_w, pool_scale, pool_out_w, fnet_out_w,
              mix_out_w, ln1_g, ln1_b, mlp_up_w, mlp_up_b, mlp_down_w, mlp_down_b, ln2_g, ln2_b):
    alpha = DEEPNORM_ALPHA
    for i in range(DEPTH):
        lp = {
            'in_w': in_w[i], 'in_b': in_b[i], 'ssd_conv_w': ssd_conv_w[i], 'ssd_conv_b': ssd_conv_b[i],
            'ssd_dt_bias': ssd_dt_bias[i], 'ssd_a_log': ssd_a_log[i], 'ssd_d': ssd_d[i],
            'ssd_norm_w': ssd_norm_w[i], 'ssd_out_w': ssd_out_w[i], 'na_rpb': na_rpb[i],
            'na_out_w': na_out_w[i], 'pool_w': pool_w[i], 'pool_scale': pool_scale[i],
            'pool_out_w': pool_out_w[i], 'fnet_out_w': fnet_out_w[i], 'mix_out_w': mix_out_w[i],
            'mlp_up_w': mlp_up_w[i], 'mlp_up_b': mlp_up_b[i], 'mlp_down_w': mlp_down_w[i],
            'mlp_down_b': mlp_down_b[i],
        }
        need_ctx = i < DEPTH - 1
        mod_x = (jax.nn.silu(c) @ ada_w[i] + ada_b[i])[:, None, :]
        mod_c = (jax.nn.silu(c_ctx) @ ada_w[i] + ada_b[i])[None, None, :]
        sh1, sc1, g1, sh2, sc2, g2 = jnp.split(mod_x, 6, axis=-1)
        csh1, csc1, cg1, csh2, csc2, cg2 = jnp.split(mod_c, 6, axis=-1)

        h = modulate(layer_norm(x), sh1, sc1)
        hc = modulate(layer_norm(ctx), csh1, csc1)
        mix_x, mix_c = token_mixing(h, hc, lp, need_ctx)

        x = layer_norm(alpha * x + g1 * mix_x, ln1_g[i], ln1_b[i])
        x = layer_norm(alpha * x + g2 * squared_relu_mlp(modulate(layer_norm(x), sh2, sc2), lp), ln2_g[i], ln2_b[i])
        if need_ctx:
            ctx = layer_norm(alpha * ctx + cg1 * mix_c, ln1_g[i], ln1_b[i])
            ctx = layer_norm(alpha * ctx + cg2 * squared_relu_mlp(modulate(layer_norm(ctx), csh2, csc2), lp),
                             ln2_g[i], ln2_b[i])
    return x
```

```python
import functools
import math

import jax
import jax.numpy as jnp
from jax import lax
from jax.experimental import pallas as pl
from jax.experimental.pallas import tpu as pltpu

F32 = jnp.float32
BF16 = jnp.bfloat16

D_MODEL = 1024
DEPTH = 4
GRID_W = 64

SSD_HEADS = 16
SSD_HEADDIM = 64
SSD_D_INNER = SSD_HEADS * SSD_HEADDIM
SSD_GROUPS = 4
SSD_STATE = 128
SSD_CONV = 5
SSD_CHUNK = 128
SSD_XBC = SSD_D_INNER + 2 * SSD_GROUPS * SSD_STATE
HEADS_PER_GROUP = SSD_HEADS // SSD_GROUPS
GROUP_WIDTH = HEADS_PER_GROUP * SSD_HEADDIM

NA_HEADS = 16
NA_HEAD_DIM = 64
NA_WIDTH = NA_HEADS * NA_HEAD_DIM
NA_WIN_ROWS = 8
NA_WIN_COLS = 16

POOL_WINDOWS = (2, 4, 8, 16)
POOL_GROUP = 192
POOL_WIDTH = 4 * POOL_GROUP

FNET_HEADS = 4
FNET_HEAD_DIM = 192
FNET_WIDTH = FNET_HEADS * FNET_HEAD_DIM

N_BRANCH = 4
D_FF = 4 * D_MODEL
ROPE_THETA = 10000.0
DEEPNORM_ALPHA = (2 * DEPTH) ** 0.25
LN_EPS = 1e-6
RMS_EPS = 1e-5

LANES = 128
HALF = LANES // 2
DT_PAD = LANES
NEG_BIG = -1e30
MIB = 1024 * 1024


def _resident(shape):
    nd = len(shape)
    return pl.BlockSpec(shape, lambda *_: (0,) * nd, pipeline_mode=pl.Buffered(1))


def _layer_resident(shape, layer):
    nd = len(shape)
    return pl.BlockSpec((None,) + tuple(shape), lambda *_: (layer,) + (0,) * nd, pipeline_mode=pl.Buffered(1))


def _params(vmem_mib, n_axes=1):
    return pltpu.CompilerParams(dimension_semantics=("arbitrary",) * n_axes, vmem_limit_bytes=vmem_mib * MIB)


def _ln(x):
    mu = jnp.mean(x, axis=-1, keepdims=True)
    xc = x - mu
    var = jnp.mean(xc * xc, axis=-1, keepdims=True)
    return xc * lax.rsqrt(var + LN_EPS)


def _silu(x):
    return x * jax.nn.sigmoid(x)


def _dot(a, b):
    return jnp.dot(a, b, preferred_element_type=F32)


def _dot_nt(a, b):
    return lax.dot_general(a, b, (((1,), (1,)), ((), ())), preferred_element_type=F32)


def _ada_kernel(c_ref, w_ref, b_ref, o_ref):
    a = _silu(c_ref[...]).astype(BF16)
    o_ref[...] = _dot(a, w_ref[...].astype(BF16)) + b_ref[...]


def _ada_all_layers(c_all, ada_w, ada_b):
    rows = c_all.shape[0]
    return pl.pallas_call(
        _ada_kernel,
        out_shape=jax.ShapeDtypeStruct((DEPTH, rows, 6 * D_MODEL), F32),
        grid=(DEPTH, 6),
        in_specs=[
            _resident((rows, D_MODEL)),
            pl.BlockSpec((None, D_MODEL, D_MODEL), lambda l, j: (l, 0, j)),
            pl.BlockSpec((None, 1, D_MODEL), lambda l, j: (l, 0, j)),
        ],
        out_specs=pl.BlockSpec((None, rows, D_MODEL), lambda l, j: (l, 0, j)),
        compiler_params=_params(32, 2),
        name="ada_mod",
    )(c_all, ada_w, ada_b.reshape(DEPTH, 1, 6 * D_MODEL))


IN_SPLITS = (SSD_D_INNER, SSD_XBC, DT_PAD, 3 * NA_WIDTH, POOL_WIDTH, FNET_WIDTH)
IN_DTYPES = (F32, F32, F32, BF16, F32, F32)


def _inproj_kernel(x_ref, mod_ref, *refs):
    n = len(IN_SPLITS)
    w_refs, b_refs, o_refs = refs[:n], refs[n:2 * n], refs[2 * n:]
    h = (_ln(x_ref[...]) * (1.0 + mod_ref[1:2, :]) + mod_ref[0:1, :]).astype(BF16)
    for w_ref, b_ref, o_ref in zip(w_refs, b_refs, o_refs):
        o_ref[...] = (_dot(h, w_ref[...]) + b_ref[...]).astype(o_ref.dtype)


def _inproj(x2d, mod, tiles_per_mod, layer, weights, biases, tm):
    m = x2d.shape[0]
    if tiles_per_mod is None:
        mod_map = lambda i: (0, 0, 0)
    else:
        mod_map = lambda i: (i // tiles_per_mod, 0, 0)
    in_specs = [pl.BlockSpec((tm, D_MODEL), lambda i: (i, 0)), pl.BlockSpec((None, 8, D_MODEL), mod_map)]
    in_specs += [_layer_resident((D_MODEL, n), layer) for n in IN_SPLITS]
    in_specs += [_layer_resident((1, n), layer) for n in IN_SPLITS]
    return pl.pallas_call(
        _inproj_kernel,
        out_shape=[jax.ShapeDtypeStruct((m, n), dt) for n, dt in zip(IN_SPLITS, IN_DTYPES)],
        grid=(m // tm,),
        in_specs=in_specs,
        out_specs=[pl.BlockSpec((tm, n), lambda i: (i, 0)) for n in IN_SPLITS],
        compiler_params=_params(52),
        name="in_proj",
    )(x2d, mod, *weights, *biases)


def _row_iota(shape):
    return lax.broadcasted_iota(jnp.int32, shape, 0)


def _lane_iota(shape):
    return lax.broadcasted_iota(jnp.int32, shape, 1)


def _shifted_rows(ref, base, d, first, last):
    t = SSD_CHUNK
    if d == 0 or not ((first and d < 0) or (last and d > 0)):
        return ref[pl.ds(base + d, t), :]
    x = ref[pl.ds(base, t), :]
    row = _row_iota(x.shape)
    if d < 0:
        return jnp.where(row >= -d, pltpu.roll(x, -d, 0), 0.0)
    return jnp.where(row < t - d, pltpu.roll(x, t - d, 0), 0.0)


def _split3(x):
    hi = x.astype(BF16)
    r1 = x - hi.astype(F32)
    mid = r1.astype(BF16)
    lo = (r1 - mid.astype(F32)).astype(BF16)
    return hi, mid, lo


def _ssd_kernel(xs0_x, xs1_x, bm_x, cm_x, dt_x, z_x, xs0_c, xs1_c, bm_c, cm_c, dt_c, z_c,
                wxs, wbm, wcm, bxs, bbm, bcm, dtb_ref, alog_ref, dch_ref, nw_ref, cos_ref, sin_ref,
                ox_ref, oc_ref,
                xs_s, b_s, bt_s, c_s, dt_s, ac_s, act_s, y_s, st_s):
    t = SSD_CHUNK
    n_x, n_c = bm_x.shape[0], bm_c.shape[0]
    nc_x, nc_c = n_x // t, n_c // t
    g = pl.program_id(1)
    half = _lane_iota((t, LANES)) < HALF
    half_row = _lane_iota((1, LANES)) < HALF
    li, si = _row_iota((t, t)), _lane_iota((t, t))
    tril = jnp.where(si <= li, 1.0, 0.0).astype(BF16)
    triu = jnp.where(si >= li, 1.0, 0.0).astype(BF16)
    is_fwd_lane = (_lane_iota((t, LANES)) % (2 * HEADS_PER_GROUP)) < HEADS_PER_GROUP
    a_row = -jnp.exp(alog_ref[...])

    def conv_silu(ref, w_ref, b_ref, base, lanes, first, last):
        acc = b_ref[:, lanes] + w_ref[0:1, lanes] * _shifted_rows(ref, base, -(SSD_CONV // 2), first, last)
        for k in range(1, SSD_CONV):
            acc = acc + w_ref[k:k + 1, lanes] * _shifted_rows(ref, base, k - SSD_CONV // 2, first, last)
        return _silu(acc)

    def prep_chunk(refs, base, u0, first, last, rope):
        xs_tiles, bm_r, cm_r, dt_r = refs
        dst = pl.ds(u0 + base, t)
        for p, xs_r in enumerate(xs_tiles):
            lanes = slice(p * LANES, (p + 1) * LANES)
            xs_s[dst, lanes] = conv_silu(xs_r, wxs, bxs, base, lanes, first, last)
        bm = conv_silu(bm_r, wbm, bbm, base, slice(None), first, last)
        cm = conv_silu(cm_r, wcm, bcm, base, slice(None), first, last)
        if rope:
            cos2, sin2 = cos_ref[pl.ds(base, t), :], sin_ref[pl.ds(base, t), :]
            bm = bm * cos2 + pltpu.roll(bm, HALF, 1) * sin2
            cm = cm * cos2 + pltpu.roll(cm, HALF, 1) * sin2
        b_s[dst, :] = bm.astype(BF16)
        bt_s[dst, :] = bm.T.astype(BF16)
        c_s[dst, :] = cm.astype(BF16)
        raw = dt_r[pl.ds(base, t), :] + dtb_ref[...]
        dt = jnp.maximum(raw, 0.0) + jnp.log1p(jnp.exp(-jnp.abs(raw)))
        dt_s[dst, :] = dt
        parts = _split3(dt * a_row)
        prefix = _dot(tril, parts[0]) + _dot(tril, parts[1]) + _dot(tril, parts[2])
        suffix = _dot(triu, parts[0]) + _dot(triu, parts[1]) + _dot(triu, parts[2])
        acum = jnp.where(is_fwd_lane, prefix, suffix)
        ac_s[dst, :] = acum
        act_s[dst, :] = acum.T

    def prep_seq(refs, n_chunks, u0, rope):
        prep_chunk(refs, 0, u0, True, False, rope)

        def body(c, carry):
            prep_chunk(refs, pl.multiple_of(c * t, t), u0, False, False, rope)
            return carry

        lax.fori_loop(1, n_chunks - 1, body, 0)
        prep_chunk(refs, (n_chunks - 1) * t, u0, False, True, rope)

    prep_seq(((xs0_c, xs1_c), bm_c, cm_c, dt_c), nc_c, 0, False)
    prep_seq(((xs0_x, xs1_x), bm_x, cm_x, dt_x), nc_x, n_c, True)

    def chunk_step(u, dirn):
        r0 = pl.multiple_of(u * t, t)
        rows = pl.ds(r0, t)
        b_c, bt_c, c_c = b_s[rows, :], bt_s[rows, :], c_s[rows, :]
        cb = _dot_nt(c_c, b_c)
        acum, dtc = ac_s[rows, :], dt_s[rows, :]
        tri = (si <= li) if dirn == 0 else (si >= li)
        end_lane = t - 1 if dirn == 0 else 0
        lane = _lane_iota((t, LANES))
        st = st_s[...]
        y_off = _dot(c_c, st.astype(BF16))
        w_tiles, cd_tiles = [], []
        for p in range(GROUP_WIDTH // LANES):
            lanes = slice(p * LANES, (p + 1) * LANES)
            acols, dcols, aends, decays = [], [], [], []
            for jj in range(2):
                ln = g * (2 * HEADS_PER_GROUP) + dirn * HEADS_PER_GROUP + 2 * p + jj
                sel = lane == ln
                acol = jnp.sum(jnp.where(sel, acum, 0.0), axis=1, keepdims=True)
                dcol = jnp.sum(jnp.where(sel, dtc, 0.0), axis=1, keepdims=True)
                arow = act_s[pl.ds(r0 + ln, 1), :]
                aend = jnp.sum(jnp.where(_lane_iota((1, t)) == end_lane, arow, 0.0), axis=1, keepdims=True)
                acols.append(acol)
                dcols.append(dcol)
                aends.append(aend)
                decays.append(jnp.exp(jnp.where(tri, acol - arow, NEG_BIG)))
            xdt = xs_s[rows, lanes] * jnp.where(half, dcols[0], dcols[1])
            xdt_b = xdt.astype(BF16)
            y0 = _dot((cb * decays[0]).astype(BF16), xdt_b)
            y1 = _dot((cb * decays[1]).astype(BF16), xdt_b)
            exp_a = jnp.where(half, jnp.exp(acols[0]), jnp.exp(acols[1]))
            y = jnp.where(half, y0, y1) + y_off[:, lanes] * exp_a
            if dirn == 0:
                y_s[rows, lanes] = y
            else:
                y_s[rows, lanes] = y_s[rows, lanes] + y
            to_end = jnp.where(half, jnp.exp(aends[0] - acols[0]), jnp.exp(aends[1] - acols[1]))
            w_tiles.append((xdt * to_end).astype(BF16))
            cd_tiles.append(jnp.where(half_row, jnp.exp(aends[0]), jnp.exp(aends[1])))
        new = _dot(bt_c, jnp.concatenate(w_tiles, axis=1))
        st_s[...] = st * jnp.concatenate(cd_tiles, axis=1) + new

    u_x = nc_c
    for dirn in range(2):
        st_s[...] = jnp.zeros_like(st_s)
        if dirn == 0:
            lax.fori_loop(0, nc_c, lambda i, c: (chunk_step(i, 0), c)[1], 0)
            lax.fori_loop(0, nc_x, lambda i, c: (chunk_step(u_x + i, 0), c)[1], 0)
        else:
            lax.fori_loop(0, nc_c, lambda i, c: (chunk_step(nc_c - 1 - i, 1), c)[1], 0)
            lax.fori_loop(0, nc_x, lambda i, c: (chunk_step(u_x + nc_x - 1 - i, 1), c)[1], 0)

    def finish(z_ref, o_ref, n_chunks, u0):
        def body(c, carry):
            base = pl.multiple_of(c * t, t)
            rows = pl.ds(u0 + base, t)
            zz = z_ref[pl.ds(base, t), :]
            y = (y_s[rows, :] + dch_ref[...] * xs_s[rows, :]) * _silu(zz)
            ms = jnp.mean(y * y, axis=-1, keepdims=True)
            o_ref[pl.ds(base, t), :] = (y * lax.rsqrt(ms + RMS_EPS) * nw_ref[...]).astype(o_ref.dtype)
            return carry

        lax.fori_loop(0, n_chunks, body, 0)

    finish(z_c, oc_ref, nc_c, 0)
    finish(z_x, ox_ref, nc_x, n_c)


def _ssd(xbc_x, dt_x, z_x, xbc_c, dt_c, z_c, conv_w, conv_b, dt_bias, a_log, d_ch, norm_w, cos2, sin2, layer):
    bsz, n_x, _ = xbc_x.shape
    n_c = xbc_c.shape[1]
    n_all = n_x + n_c
    gw, st = GROUP_WIDTH, SSD_STATE
    b_blk0 = SSD_D_INNER // st
    c_blk0 = b_blk0 + SSD_GROUPS

    def seq_specs(n):
        return [
            pl.BlockSpec((None, n, LANES), lambda b, g: (b, 0, 2 * g)),
            pl.BlockSpec((None, n, LANES), lambda b, g: (b, 0, 2 * g + 1)),
            pl.BlockSpec((None, n, st), lambda b, g: (b, 0, b_blk0 + g)),
            pl.BlockSpec((None, n, st), lambda b, g: (b, 0, c_blk0 + g)),
            pl.BlockSpec((None, n, DT_PAD), lambda b, g: (b, 0, 0)),
            pl.BlockSpec((None, n, gw), lambda b, g: (b, 0, g)),
        ]

    in_specs = seq_specs(n_x) + seq_specs(n_c) + [
        pl.BlockSpec((None, 8, gw), lambda b, g: (layer, 0, g)),
        pl.BlockSpec((None, 8, st), lambda b, g: (layer, 0, b_blk0 + g)),
        pl.BlockSpec((None, 8, st), lambda b, g: (layer, 0, c_blk0 + g)),
        pl.BlockSpec((None, 1, gw), lambda b, g: (layer, 0, g)),
        pl.BlockSpec((None, 1, st), lambda b, g: (layer, 0, b_blk0 + g)),
        pl.BlockSpec((None, 1, st), lambda b, g: (layer, 0, c_blk0 + g)),
        pl.BlockSpec((None, 1, DT_PAD), lambda b, g: (layer, 0, 0)),
        pl.BlockSpec((None, 1, DT_PAD), lambda b, g: (layer, 0, 0)),
        pl.BlockSpec((None, 1, gw), lambda b, g: (layer, 0, g)),
        pl.BlockSpec((None, 1, gw), lambda b, g: (layer, 0, g)),
        _resident((n_x, st)),
        _resident((n_x, st)),
    ]
    out_shape = [jax.ShapeDtypeStruct((bsz, n_x, SSD_D_INNER), BF16),
                 jax.ShapeDtypeStruct((bsz, n_c, SSD_D_INNER), BF16)]
    out_specs = [pl.BlockSpec((None, n_x, gw), lambda b, g: (b, 0, g)),
                 pl.BlockSpec((None, n_c, gw), lambda b, g: (b, 0, g))]
    scratch = [
        pltpu.VMEM((n_all, gw), F32),
        pltpu.VMEM((n_all, st), BF16),
        pltpu.VMEM((n_all, st), BF16),
        pltpu.VMEM((n_all, st), BF16),
        pltpu.VMEM((n_all, DT_PAD), F32),
        pltpu.VMEM((n_all, DT_PAD), F32),
        pltpu.VMEM((n_all, DT_PAD), F32),
        pltpu.VMEM((n_all, gw), F32),
        pltpu.VMEM((st, gw), F32),
    ]
    return pl.pallas_call(
        _ssd_kernel,
        out_shape=out_shape,
        grid=(bsz, SSD_GROUPS),
        in_specs=in_specs,
        out_specs=out_specs,
        scratch_shapes=scratch,
        compiler_params=_params(48, 2),
        name="ssd_mixer",
    )(xbc_x, xbc_x, xbc_x, xbc_x, dt_x, z_x, xbc_c, xbc_c, xbc_c, xbc_c, dt_c, z_c,
      conv_w, conv_w, conv_w, conv_b, conv_b, conv_b, dt_bias, a_log, d_ch, norm_w, cos2, sin2)


NA_PAIR = 2
NA_DR = 2 * NA_WIN_ROWS


def _softmax_pv(scores, values):
    m = functools.reduce(jnp.maximum, [jnp.max(s, axis=-1, keepdims=True) for s in scores])
    ps = [jnp.exp(s - m) for s in scores]
    denom = functools.reduce(jnp.add, [jnp.sum(p, axis=-1, keepdims=True) for p in ps])
    o = functools.reduce(jnp.add, [_dot(p.astype(BF16), v) for p, v in zip(ps, values)])
    return o / denom


def _na_kernel(q_ref, k_ref, v_ref, qc_ref, kc_ref, vc_ref, bias_ref, ox_ref, oc_ref):
    n = q_ref.shape[0]
    rows = n // GRID_W
    win = NA_WIN_ROWS * GRID_W
    k_c, v_c = kc_ref[...], vc_ref[...]

    def head_masks(shape):
        lane = _lane_iota(shape)
        return lane < HALF, lane >= HALF

    def row_body(r, carry):
        rs = jnp.clip(r - NA_WIN_ROWS // 2, 0, rows - NA_WIN_ROWS)
        q = q_ref[pl.ds(pl.multiple_of(r * GRID_W, GRID_W), GRID_W), :]
        kstart = pl.multiple_of(rs * GRID_W, GRID_W)
        k_w, v_w = k_ref[pl.ds(kstart, win), :], v_ref[pl.ds(kstart, win), :]
        dr0 = rs - r + NA_WIN_ROWS - 1
        outs = []
        for hh, keep in enumerate(head_masks(q.shape)):
            qz = jnp.where(keep, q, jnp.zeros_like(q))
            bias = jnp.concatenate([bias_ref[hh, dr0 + 2 * i] for i in range(NA_WIN_ROWS // 2)], axis=1)
            s_loc = _dot_nt(qz, k_w) + bias
            s_ctx = _dot_nt(qz, k_c)
            outs.append(_softmax_pv([s_loc, s_ctx], [v_w, v_c]))
        first = _lane_iota(outs[0].shape) < HALF
        ox_ref[pl.ds(pl.multiple_of(r * GRID_W, GRID_W), GRID_W), :] = jnp.where(first, outs[0], outs[1]).astype(ox_ref.dtype)
        return carry

    lax.fori_loop(0, rows, row_body, 0)

    q_c = qc_ref[...]
    outs = []
    for keep in head_masks(q_c.shape):
        qz = jnp.where(keep, q_c, jnp.zeros_like(q_c))
        outs.append(_softmax_pv([_dot_nt(qz, k_c)], [v_c]))
    first = _lane_iota(outs[0].shape) < HALF
    oc_ref[...] = jnp.where(first, outs[0], outs[1]).astype(oc_ref.dtype)


def _na(qkv_x, qkv_c, bias_tab, layer):
    bsz, n_x, _ = qkv_x.shape
    n_c = qkv_c.shape[1]
    n_pairs = NA_HEADS // NA_PAIR
    blk = lambda n, off: pl.BlockSpec((None, n, LANES), lambda b, p: (b, 0, off + p))
    return pl.pallas_call(
        _na_kernel,
        out_shape=[jax.ShapeDtypeStruct((bsz, n_x, NA_WIDTH), BF16), jax.ShapeDtypeStruct((bsz, n_c, NA_WIDTH), BF16)],
        grid=(bsz, n_pairs),
        in_specs=[blk(n_x, 0), blk(n_x, n_pairs), blk(n_x, 2 * n_pairs),
                  blk(n_c, 0), blk(n_c, n_pairs), blk(n_c, 2 * n_pairs),
                  pl.BlockSpec((None, NA_PAIR, NA_DR, GRID_W, LANES), lambda b, p: (layer, p, 0, 0, 0))],
        out_specs=[blk(n_x, 0), blk(n_c, 0)],
        compiler_params=_params(32, 2),
        name="na_mixer",
    )(qkv_x, qkv_x, qkv_x, qkv_c, qkv_c, qkv_c, bias_tab)


def _na_bias_table(na_rpb):
    j = jnp.arange(GRID_W)[:, None]
    kc = jnp.arange(GRID_W)[None, :]
    start = jnp.clip(j - NA_WIN_COLS // 2, 0, GRID_W - NA_WIN_COLS)
    valid = (kc >= start) & (kc < start + NA_WIN_COLS)
    dc = jnp.clip(kc - j + NA_WIN_COLS - 1, 0, 2 * NA_WIN_COLS - 2)
    tab = jnp.where(valid[None, None, None], na_rpb[:, :, :, dc], NEG_BIG)
    tab = jnp.pad(tab, ((0, 0), (0, 0), (0, NA_DR + 1 - tab.shape[2]), (0, 0), (0, 0)))
    return jnp.concatenate([tab[:, :, :NA_DR], tab[:, :, 1:NA_DR + 1]], axis=-1).astype(F32)


POOL_TILES = POOL_WIDTH // LANES


def _pool_kernel(*refs):
    u_tiles, (w_ref, scale_ref, o_ref) = refs[:POOL_TILES], refs[POOL_TILES:]
    t = SSD_CHUNK
    n = o_ref.shape[0]
    n_chunks = n // t

    def tile_windows(p):
        lo_g, hi_g = (p * LANES) // POOL_GROUP, ((p + 1) * LANES - 1) // POOL_GROUP
        split = (hi_g * POOL_GROUP) - p * LANES if hi_g != lo_g else None
        return POOL_WINDOWS[lo_g], POOL_WINDOWS[hi_g], split

    def chunk(base, first, last):
        pos = base + _row_iota((t, 1))
        tiles = []
        for p, u_ref in enumerate(u_tiles):
            w_lo, w_hi, split = tile_windows(p)
            x0 = u_ref[pl.ds(base, t), :]
            sums, acc, have = {}, x0, 1
            for w in POOL_WINDOWS:
                if w > w_hi:
                    break
                for d in list(range(-(w // 2), -(have // 2))) + list(range(have - have // 2, w // 2)):
                    acc = acc + _shifted_rows(u_ref, base, d, first, last)
                have = w
                cnt = (jnp.minimum(pos + w // 2, n) - jnp.maximum(pos - w // 2, 0)).astype(F32)
                sums[w] = acc / cnt
            if split is None:
                pooled = sums[w_lo]
            else:
                pooled = jnp.where(_lane_iota((t, LANES)) < split, sums[w_lo], sums[w_hi])
            tiles.append((pooled - x0).astype(BF16))
        mapped = _dot(jnp.concatenate(tiles, axis=1), w_ref[...])
        o_ref[pl.ds(base, t), :] = (mapped * scale_ref[...]).astype(o_ref.dtype)

    chunk(0, True, False)

    def body(c, carry):
        chunk(pl.multiple_of(c * t, t), False, False)
        return carry

    lax.fori_loop(1, n_chunks - 1, body, 0)
    chunk((n_chunks - 1) * t, False, True)


def _pool(u, w_bd, scale, layer):
    bsz, n, _ = u.shape
    return pl.pallas_call(
        _pool_kernel,
        out_shape=jax.ShapeDtypeStruct((bsz, n, POOL_WIDTH), BF16),
        grid=(bsz,),
        in_specs=[pl.BlockSpec((None, n, LANES), lambda b, p=p: (b, 0, p)) for p in range(POOL_TILES)]
        + [_layer_resident((POOL_WIDTH, POOL_WIDTH), layer), _layer_resident((1, POOL_WIDTH), layer)],
        out_specs=pl.BlockSpec((None, n, POOL_WIDTH), lambda b: (b, 0, 0)),
        compiler_params=_params(40),
        name="pool_mixer",
    )(*([u] * POOL_TILES), w_bd, scale)


def _fnet_kernel(u_ref, cc_ref, sc_ref, pos_ref, o_ref, z_s):
    n = u_ref.shape[0]
    rows = 256

    @pl.when(pl.program_id(1) == 0)
    def _():
        for r in range(n // rows):
            ub = u_ref[r * rows:(r + 1) * rows, :].astype(BF16)
            z_s[r * rows:(r + 1) * rows, :] = _dot(ub, cc_ref[...]).astype(BF16)
            z_s[n + r * rows:n + (r + 1) * rows, :] = _dot(ub, sc_ref[...]).astype(BF16)

    scale = 1.0 / math.sqrt(n * FNET_HEAD_DIM)
    o_ref[...] = (_dot(pos_ref[...], z_s[...]) * scale).astype(o_ref.dtype)


def _fnet(u, chan_cos, chan_sin, pos_tab, tq):
    bsz, n, _ = u.shape
    return pl.pallas_call(
        _fnet_kernel,
        out_shape=jax.ShapeDtypeStruct((bsz, n, FNET_WIDTH), BF16),
        grid=(bsz, n // tq),
        in_specs=[pl.BlockSpec((None, n, FNET_WIDTH), lambda b, j: (b, 0, 0)),
                  _resident((FNET_WIDTH, FNET_WIDTH)),
                  _resident((FNET_WIDTH, FNET_WIDTH)),
                  pl.BlockSpec((tq, 2 * n), lambda b, j: (j, 0))],
        out_specs=pl.BlockSpec((None, tq, FNET_WIDTH), lambda b, j: (b, j, 0)),
        scratch_shapes=[pltpu.VMEM((2 * n, FNET_WIDTH), BF16)],
        compiler_params=_params(48, 2),
        name="fnet_mixer",
    )(u, chan_cos, chan_sin, pos_tab)


def _dft_cos_sin(n):
    k = jnp.arange(n, dtype=jnp.int32)
    ang = ((k[:, None] * k[None, :]) % n).astype(F32) * (2.0 * math.pi / n)
    return jnp.cos(ang), jnp.sin(ang)


def _fnet_tables(n):
    c, s = _dft_cos_sin(n)
    return jnp.concatenate([c, -s], axis=1).astype(BF16)


def _fnet_channel_tables():
    c, s = _dft_cos_sin(FNET_HEAD_DIM)
    eye = jnp.eye(FNET_HEADS, dtype=F32)
    return jnp.kron(eye, c).astype(BF16), jnp.kron(eye, s).astype(BF16)


def _merge_kernel(x_ref, mod_ref, ssd_ref, na_ref, pool_ref, fnet_ref,
                  wg_ref, bg_ref, wssd_ref, wna_ref, wpool_ref, wfnet_ref, wmix_ref, g_ref, b_ref, o_ref):
    x = x_ref[...]
    h = (_ln(x) * (1.0 + mod_ref[1:2, :]) + mod_ref[0:1, :]).astype(BF16)
    merged = None
    branches = ((ssd_ref, wssd_ref), (na_ref, wna_ref), (pool_ref, wpool_ref), (fnet_ref, wfnet_ref))
    for i, (y_ref, w_ref) in enumerate(branches):
        cols = slice(i * D_MODEL, (i + 1) * D_MODEL)
        gate = jax.nn.sigmoid(_dot(h, wg_ref[:, cols]) + bg_ref[:, cols])
        term = gate * _dot(y_ref[...], w_ref[...])
        merged = term if merged is None else merged + term
    mix = _dot(merged.astype(BF16), wmix_ref[...])
    y = _ln(DEEPNORM_ALPHA * x + mod_ref[2:3, :] * mix)
    o_ref[...] = y * g_ref[...] + b_ref[...]


def _merge(x2d, mod, tiles_per_mod, layer, ssd, na, pool, fnet, weights, tm):
    m = x2d.shape[0]
    mod_map = (lambda i: (0, 0, 0)) if tiles_per_mod is None else (lambda i: (i // tiles_per_mod, 0, 0))
    tok = lambda n: pl.BlockSpec((tm, n), lambda i: (i, 0))
    wg, bg, wssd, wna, wpool, wfnet, wmix, ln_g, ln_b = weights
    in_specs = [tok(D_MODEL), pl.BlockSpec((None, 8, D_MODEL), mod_map),
                tok(SSD_D_INNER), tok(NA_WIDTH), tok(POOL_WIDTH), tok(FNET_WIDTH),
                _layer_resident((D_MODEL, N_BRANCH * D_MODEL), layer),
                _layer_resident((1, N_BRANCH * D_MODEL), layer),
                _layer_resident((SSD_D_INNER, D_MODEL), layer),
                _layer_resident((NA_WIDTH, D_MODEL), layer),
                _layer_resident((POOL_WIDTH, D_MODEL), layer),
                _layer_resident((FNET_WIDTH, D_MODEL), layer),
                _layer_resident((D_MODEL, D_MODEL), layer),
                _layer_resident((1, D_MODEL), layer),
                _layer_resident((1, D_MODEL), layer)]
    return pl.pallas_call(
        _merge_kernel,
        out_shape=jax.ShapeDtypeStruct((m, D_MODEL), F32),
        grid=(m // tm,),
        in_specs=in_specs,
        out_specs=tok(D_MODEL),
        compiler_params=_params(52),
        name="merge_mix",
    )(x2d, mod, ssd, na, pool, fnet, wg, bg, wssd, wna, wpool, wfnet, wmix, ln_g, ln_b)


MLP_CHUNK = 1024


def _mlp_kernel(x_ref, mod_ref, wup_ref, bup_ref, wdn_ref, bdn_ref, g_ref, b_ref, o_ref):
    x = x_ref[...]
    h = (_ln(x) * (1.0 + mod_ref[4:5, :]) + mod_ref[3:4, :]).astype(BF16)
    acc = None
    for c in range(D_FF // MLP_CHUNK):
        cols = slice(c * MLP_CHUNK, (c + 1) * MLP_CHUNK)
        a = jnp.maximum(_dot(h, wup_ref[:, cols]) + bup_ref[:, cols], 0.0)
        part = _dot((a * a).astype(BF16), wdn_ref[cols, :])
        acc = part if acc is None else acc + part
    y = _ln(DEEPNORM_ALPHA * x + mod_ref[5:6, :] * (acc + bdn_ref[...]))
    o_ref[...] = y * g_ref[...] + b_ref[...]


def _mlp(x2d, mod, tiles_per_mod, layer, weights, tm):
    m = x2d.shape[0]
    mod_map = (lambda i: (0, 0, 0)) if tiles_per_mod is None else (lambda i: (i // tiles_per_mod, 0, 0))
    tok = pl.BlockSpec((tm, D_MODEL), lambda i: (i, 0))
    wup, bup, wdn, bdn, ln_g, ln_b = weights
    in_specs = [tok, pl.BlockSpec((None, 8, D_MODEL), mod_map),
                _layer_resident((D_MODEL, D_FF), layer), _layer_resident((1, D_FF), layer),
                _layer_resident((D_FF, D_MODEL), layer), _layer_resident((1, D_MODEL), layer),
                _layer_resident((1, D_MODEL), layer), _layer_resident((1, D_MODEL), layer)]
    return pl.pallas_call(
        _mlp_kernel,
        out_shape=jax.ShapeDtypeStruct((m, D_MODEL), F32),
        grid=(m // tm,),
        in_specs=in_specs,
        out_specs=tok,
        compiler_params=_params(52),
        name="mlp",
    )(x2d, mod, wup, bup, wdn, bdn, ln_g, ln_b)


def _rope_tables(n):
    pos = jnp.arange(n)
    quarter = SSD_STATE // 4
    inv_freq = ROPE_THETA ** (-jnp.arange(quarter, dtype=F32) / quarter)
    row = (pos // GRID_W).astype(F32)[:, None] * inv_freq
    col = (pos % GRID_W).astype(F32)[:, None] * inv_freq
    ang = jnp.concatenate([row, col], -1)
    cos, sin = jnp.cos(ang), jnp.sin(ang)
    return jnp.concatenate([cos, cos], -1), jnp.concatenate([-sin, sin], -1)


def _dt_lane_layout(a):
    lead = a.shape[:-2]
    a = a.reshape(lead + (2, SSD_GROUPS, HEADS_PER_GROUP))
    a = jnp.swapaxes(a, -3, -2).reshape(lead + (2 * SSD_HEADS,))
    return jnp.pad(a, [(0, 0)] * len(lead) + [(0, DT_PAD - 2 * SSD_HEADS)])


def _block_diag(w):
    l, g, c, _ = w.shape
    eye = jnp.eye(g, dtype=w.dtype)
    return jnp.einsum("lgcd,gh->lgchd", w, eye).reshape(l, g * c, g * c)


def kernel(x, c, ctx, c_ctx, ada_w, ada_b, in_w, in_b, ssd_conv_w, ssd_conv_b, ssd_dt_bias, ssd_a_log, ssd_d,
           ssd_norm_w, ssd_out_w, na_rpb, na_out_w, pool_w, pool_scale, pool_out_w, fnet_out_w, mix_out_w,
           ln1_g, ln1_b, mlp_up_w, mlp_up_b, mlp_down_w, mlp_down_b, ln2_g, ln2_b):
    bsz, n_x, d = x.shape
    n_c = ctx.shape[1]
    depth = in_w.shape[0]

    offs = [0]
    for wdt in (SSD_D_INNER, SSD_XBC, 2 * SSD_HEADS, 3 * NA_WIDTH, POOL_WIDTH, FNET_WIDTH, N_BRANCH * D_MODEL):
        offs.append(offs[-1] + wdt)

    def in_cols(a, i):
        return a[..., offs[i]:offs[i + 1]]

    def split_in(a):
        dt_cols = in_cols(a, 2)
        dt_cols = _dt_lane_layout(dt_cols.reshape(dt_cols.shape[:-1] + (2, SSD_HEADS)))
        qkv = in_cols(a, 3)
        q_scale = jnp.concatenate([jnp.full((NA_WIDTH,), NA_HEAD_DIM ** -0.5, F32), jnp.ones((2 * NA_WIDTH,), F32)])
        return [in_cols(a, 0), in_cols(a, 1), dt_cols, qkv * q_scale, in_cols(a, 4), in_cols(a, 5)], in_cols(a, 6)

    w_parts, w_gate = split_in(in_w)
    b_parts, b_gate = split_in(in_b[:, None, :])
    w_parts = [w.astype(BF16) for w in w_parts]
    w_gate = w_gate.astype(BF16)

    conv_w = jnp.pad(ssd_conv_w, ((0, 0), (0, 8 - SSD_CONV), (0, 0)))
    conv_b = ssd_conv_b[:, None, :]
    dt_bias = _dt_lane_layout(ssd_dt_bias)[:, None, :]
    a_log = _dt_lane_layout(ssd_a_log)[:, None, :]
    d_ch = jnp.repeat(ssd_d, SSD_HEADDIM, axis=-1)[:, None, :]
    norm_w = ssd_norm_w[:, None, :]
    cos2, sin2 = _rope_tables(n_x)

    bias_tab = _na_bias_table(na_rpb)
    pool_bd = _block_diag(pool_w).astype(BF16)
    pool_sc = pool_scale[:, None, :]
    chan_cos, chan_sin = _fnet_channel_tables()
    pos_tab_x = _fnet_tables(n_x)
    pos_tab_c = _fnet_tables(n_c)

    merge_w = [w_gate, b_gate, ssd_out_w.astype(BF16), na_out_w.astype(BF16), pool_out_w.astype(BF16),
               fnet_out_w.astype(BF16), mix_out_w.astype(BF16), ln1_g[:, None, :], ln1_b[:, None, :]]
    mlp_w = [mlp_up_w.astype(BF16), mlp_up_b[:, None, :], mlp_down_w.astype(BF16), mlp_down_b[:, None, :],
             ln2_g[:, None, :], ln2_b[:, None, :]]

    rows = ((bsz + 1 + 7) // 8) * 8
    c_all = jnp.zeros((rows, d), F32).at[:bsz].set(c).at[bsz].set(c_ctx)
    mods = _ada_all_layers(c_all, ada_w, ada_b).reshape(depth, rows, 6, d)
    mods = jnp.pad(mods, ((0, 0), (0, 0), (0, 2), (0, 0)))

    tm = 256
    tiles_x = n_x // tm
    xf = x.reshape(bsz * n_x, d)
    cf = ctx.reshape(bsz * n_c, d)
    for i in range(depth):
        need_ctx = i < depth - 1
        mod_x, mod_c = mods[i, :bsz], mods[i, bsz:bsz + 1]
        z_x, xbc_x, dt_x, qkv_x, pool_x, fnet_x = _inproj(xf, mod_x, tiles_x, i, w_parts, b_parts, tm)
        z_c, xbc_c, dt_c, qkv_c, pool_c, fnet_c = _inproj(cf, mod_c, None, i, w_parts, b_parts, tm)
        r3 = lambda a, n: a.reshape(bsz, n, a.shape[-1])
        ssd_x, ssd_c = _ssd(r3(xbc_x, n_x), r3(dt_x, n_x), r3(z_x, n_x), r3(xbc_c, n_c), r3(dt_c, n_c), r3(z_c, n_c),
                            conv_w, conv_b, dt_bias, a_log, d_ch, norm_w, cos2, sin2, i)
        na_x, na_c = _na(r3(qkv_x, n_x), r3(qkv_c, n_c), bias_tab, i)
        pm_x = _pool(r3(pool_x, n_x), pool_bd, pool_sc, i)
        fm_x = _fnet(r3(fnet_x, n_x), chan_cos, chan_sin, pos_tab_x, 512)
        f2 = lambda a: a.reshape(-1, a.shape[-1])
        x1 = _merge(xf, mod_x, tiles_x, i, f2(ssd_x), f2(na_x), f2(pm_x), f2(fm_x), merge_w, tm)
        xf = _mlp(x1, mod_x, tiles_x, i, mlp_w, tm)
        if need_ctx:
            pm_c = _pool(r3(pool_c, n_c), pool_bd, pool_sc, i)
            fm_c = _fnet(r3(fnet_c, n_c), chan_cos, chan_sin, pos_tab_c, n_c)
            c1 = _merge(cf, mod_c, None, i, f2(ssd_c), f2(na_c), f2(pm_c), f2(fm_c), merge_w, tm)
            cf = _mlp(c1, mod_c, None, i, mlp_w, tm)
    return xf.reshape(bsz, n_x, d)
```

```python
import functools
import math

import jax
import jax.numpy as jnp
from jax import lax
from jax.experimental import pallas as pl
from jax.experimental.pallas import tpu as pltpu

F32 = jnp.float32
BF16 = jnp.bfloat16

D_MODEL = 1024
DEPTH = 4
GRID_W = 64

SSD_HEADS = 16
SSD_HEADDIM = 64
SSD_D_INNER = SSD_HEADS * SSD_HEADDIM
SSD_GROUPS = 4
SSD_STATE = 128
SSD_CONV = 5
SSD_CHUNK = 128
SSD_XBC = SSD_D_INNER + 2 * SSD_GROUPS * SSD_STATE
HEADS_PER_GROUP = SSD_HEADS // SSD_GROUPS
GROUP_WIDTH = HEADS_PER_GROUP * SSD_HEADDIM

NA_HEADS = 16
NA_HEAD_DIM = 64
NA_WIDTH = NA_HEADS * NA_HEAD_DIM
NA_WIN_ROWS = 8
NA_WIN_COLS = 16

POOL_WINDOWS = (2, 4, 8, 16)
POOL_GROUP = 192
POOL_WIDTH = 4 * POOL_GROUP

FNET_HEADS = 4
FNET_HEAD_DIM = 192
FNET_WIDTH = FNET_HEADS * FNET_HEAD_DIM

N_BRANCH = 4
D_FF = 4 * D_MODEL
ROPE_THETA = 10000.0
DEEPNORM_ALPHA = (2 * DEPTH) ** 0.25
LN_EPS = 1e-6
RMS_EPS = 1e-5

LANES = 128
HALF = LANES // 2
DT_PAD = LANES
NEG_BIG = -1e30
LOG2_E = 1.4426950408889634
MIB = 1024 * 1024


def _resident(shape):
    nd = len(shape)
    return pl.BlockSpec(shape, lambda *_: (0,) * nd, pipeline_mode=pl.Buffered(1))


def _layer_resident(shape, layer):
    nd = len(shape)
    return pl.BlockSpec((None,) + tuple(shape), lambda *_: (layer,) + (0,) * nd, pipeline_mode=pl.Buffered(1))


def _params(vmem_mib, n_axes=1):
    return pltpu.CompilerParams(dimension_semantics=("arbitrary",) * n_axes, vmem_limit_bytes=vmem_mib * MIB)


def _ln(x):
    mu = jnp.mean(x, axis=-1, keepdims=True)
    xc = x - mu
    var = jnp.mean(xc * xc, axis=-1, keepdims=True)
    return xc * lax.rsqrt(var + LN_EPS)


def _silu(x):
    return x * jax.nn.sigmoid(x)


def _dot(a, b):
    return jnp.dot(a, b, preferred_element_type=F32)


def _dot_nt(a, b):
    return lax.dot_general(a, b, (((1,), (1,)), ((), ())), preferred_element_type=F32)


def _ada_kernel(c_ref, w_ref, b_ref, o_ref):
    a = _silu(c_ref[...]).astype(BF16)
    o_ref[...] = _dot(a, w_ref[...].astype(BF16)) + b_ref[...]


def _ada_all_layers(c_all, ada_w, ada_b):
    rows = c_all.shape[0]
    return pl.pallas_call(
        _ada_kernel,
        out_shape=jax.ShapeDtypeStruct((DEPTH, rows, 6 * D_MODEL), F32),
        grid=(DEPTH, 6),
        in_specs=[
            _resident((rows, D_MODEL)),
            pl.BlockSpec((None, D_MODEL, D_MODEL), lambda l, j: (l, 0, j)),
            pl.BlockSpec((None, 1, D_MODEL), lambda l, j: (l, 0, j)),
        ],
        out_specs=pl.BlockSpec((None, rows, D_MODEL), lambda l, j: (l, 0, j)),
        compiler_params=_params(32, 2),
        name="ada_mod",
    )(c_all, ada_w, ada_b.reshape(DEPTH, 1, 6 * D_MODEL))


IN_SPLITS = (SSD_D_INNER, SSD_XBC, DT_PAD, 3 * NA_WIDTH, POOL_WIDTH, FNET_WIDTH)
IN_DTYPES = (F32, F32, F32, BF16, F32, F32)


def _inproj_kernel(x_ref, mod_ref, *refs):
    n = len(IN_SPLITS)
    w_refs, b_refs, o_refs = refs[:n], refs[n:2 * n], refs[2 * n:]
    h = (_ln(x_ref[...]) * (1.0 + mod_ref[1:2, :]) + mod_ref[0:1, :]).astype(BF16)
    for w_ref, b_ref, o_ref in zip(w_refs, b_refs, o_refs):
        o_ref[...] = (_dot(h, w_ref[...]) + b_ref[...]).astype(o_ref.dtype)


def _inproj(x2d, mod, tiles_per_mod, layer, weights, biases, tm):
    m = x2d.shape[0]
    if tiles_per_mod is None:
        mod_map = lambda i: (0, 0, 0)
    else:
        mod_map = lambda i: (i // tiles_per_mod, 0, 0)
    in_specs = [pl.BlockSpec((tm, D_MODEL), lambda i: (i, 0)), pl.BlockSpec((None, 8, D_MODEL), mod_map)]
    in_specs += [_layer_resident((D_MODEL, n), layer) for n in IN_SPLITS]
    in_specs += [_layer_resident((1, n), layer) for n in IN_SPLITS]
    return pl.pallas_call(
        _inproj_kernel,
        out_shape=[jax.ShapeDtypeStruct((m, n), dt) for n, dt in zip(IN_SPLITS, IN_DTYPES)],
        grid=(m // tm,),
        in_specs=in_specs,
        out_specs=[pl.BlockSpec((tm, n), lambda i: (i, 0)) for n in IN_SPLITS],
        compiler_params=_params(52),
        name="in_proj",
    )(x2d, mod, *weights, *biases)


def _row_iota(shape):
    return lax.broadcasted_iota(jnp.int32, shape, 0)


def _lane_iota(shape):
    return lax.broadcasted_iota(jnp.int32, shape, 1)


def _shifted_rows(ref, base, d, first, last):
    t = SSD_CHUNK
    if d == 0 or not ((first and d < 0) or (last and d > 0)):
        return ref[pl.ds(base + d, t), :]
    x = ref[pl.ds(base, t), :]
    row = _row_iota(x.shape)
    if d < 0:
        return jnp.where(row >= -d, pltpu.roll(x, -d, 0), 0.0)
    return jnp.where(row < t - d, pltpu.roll(x, t - d, 0), 0.0)


def _split3(x):
    hi = x.astype(BF16)
    r1 = x - hi.astype(F32)
    mid = r1.astype(BF16)
    lo = (r1 - mid.astype(F32)).astype(BF16)
    return hi, mid, lo


def _ssd_kernel(xs0_x, xs1_x, bm_x, cm_x, dt_x, z_x, xs0_c, xs1_c, bm_c, cm_c, dt_c, z_c,
                wxs, wbm, wcm, bxs, bbm, bcm, dtb_ref, alog_ref, dch_ref, nw_ref, cos_ref, sin_ref,
                ox_ref, oc_ref,
                xs_s, b_s, bt_s, c_s, dtt_s, ac_s, act_s, y_s, ea_s, sc_s, cd_s, st_s):
    t = SSD_CHUNK
    n_x, n_c = bm_x.shape[0], bm_c.shape[0]
    nc_x, nc_c = n_x // t, n_c // t
    g = pl.program_id(1)
    half = _lane_iota((t, LANES)) < HALF
    half_row = _lane_iota((1, LANES)) < HALF
    li, si = _row_iota((t, t)), _lane_iota((t, t))
    tril = jnp.where(si <= li, 1.0, 0.0).astype(BF16)
    triu = jnp.where(si >= li, 1.0, 0.0).astype(BF16)
    is_fwd_lane = (_lane_iota((t, LANES)) % (2 * HEADS_PER_GROUP)) < HEADS_PER_GROUP
    a_row = -jnp.exp(alog_ref[...])

    def conv_silu(ref, w_ref, b_ref, base, lanes, first, last):
        acc = b_ref[:, lanes] + w_ref[0:1, lanes] * _shifted_rows(ref, base, -(SSD_CONV // 2), first, last)
        for k in range(1, SSD_CONV):
            acc = acc + w_ref[k:k + 1, lanes] * _shifted_rows(ref, base, k - SSD_CONV // 2, first, last)
        return _silu(acc)

    def prep_chunk(refs, base, u0, first, last, rope):
        xs_tiles, bm_r, cm_r = refs
        dst = pl.ds(u0 + base, t)
        for p, xs_r in enumerate(xs_tiles):
            lanes = slice(p * LANES, (p + 1) * LANES)
            xs_s[dst, lanes] = conv_silu(xs_r, wxs, bxs, base, lanes, first, last)
        bm = conv_silu(bm_r, wbm, bbm, base, slice(None), first, last)
        cm = conv_silu(cm_r, wcm, bcm, base, slice(None), first, last)
        if rope:
            cos2, sin2 = cos_ref[pl.ds(base, t), :], sin_ref[pl.ds(base, t), :]
            bm = bm * cos2 + pltpu.roll(bm, HALF, 1) * sin2
            cm = cm * cos2 + pltpu.roll(cm, HALF, 1) * sin2
        b_s[dst, :] = bm.astype(BF16)
        bt_s[dst, :] = bm.T
        c_s[dst, :] = cm.astype(BF16)

    def prep_dt(dt_r, n_chunks, u0):
        def body(c, carry):
            base = pl.multiple_of(c * t, t)
            dst = pl.ds(u0 + base, t)
            raw = dt_r[pl.ds(base, t), :] + dtb_ref[...]
            dt = jnp.maximum(raw, 0.0) + jnp.log1p(jnp.exp(-jnp.abs(raw)))
            dtt_s[dst, :] = dt.T
            parts = _split3(dt * a_row)
            prefix = _dot(tril, parts[0]) + _dot(tril, parts[1]) + _dot(tril, parts[2])
            suffix = _dot(triu, parts[0]) + _dot(triu, parts[1]) + _dot(triu, parts[2])
            acum = jnp.where(is_fwd_lane, prefix, suffix) * LOG2_E
            ac_s[dst, :] = acum
            act_s[dst, :] = acum.T
            return carry

        lax.fori_loop(0, n_chunks, body, 0, unroll=2)

    @pl.when(g == 0)
    def _():
        prep_dt(dt_c, nc_c, 0)
        prep_dt(dt_x, nc_x, n_c)

    def prep_seq(refs, n_chunks, u0, rope):
        prep_chunk(refs, 0, u0, True, False, rope)

        def body(c, carry):
            prep_chunk(refs, pl.multiple_of(c * t, t), u0, False, False, rope)
            return carry

        lax.fori_loop(1, n_chunks - 1, body, 0)
        prep_chunk(refs, (n_chunks - 1) * t, u0, False, True, rope)

    prep_seq(((xs0_c, xs1_c), bm_c, cm_c), nc_c, 0, False)
    prep_seq(((xs0_x, xs1_x), bm_x, cm_x), nc_x, n_c, True)

    n_chunks = nc_c + nc_x

    def local_chunk(u, carry):
        r0 = pl.multiple_of(u * t, t)
        rows = pl.ds(r0, t)
        cb = _dot_nt(c_s[rows, :], b_s[rows, :])
        acum, bt = ac_s[rows, :], bt_s[rows, :]
        lane = _lane_iota((t, LANES))
        x_stacked = []
        for p in range(GROUP_WIDTH // LANES):
            xb = xs_s[rows, p * LANES:(p + 1) * LANES].astype(BF16)
            zero = jnp.zeros_like(xb)
            x_stacked.append(jnp.concatenate([jnp.where(half, xb, zero), jnp.where(half, zero, xb)], axis=0))
        y_tiles = [None] * (GROUP_WIDTH // LANES)
        for dirn in range(2):
            tri = (si <= li) if dirn == 0 else (si >= li)
            end_lane = t - 1 if dirn == 0 else 0
            for p in range(GROUP_WIDTH // LANES):
                lanes = slice(p * LANES, (p + 1) * LANES)
                acols, aends, m_heads, bt_heads = [], [], [], []
                for jj in range(2):
                    ln = g * (2 * HEADS_PER_GROUP) + dirn * HEADS_PER_GROUP + 2 * p + jj
                    acol = jnp.sum(jnp.where(lane == ln, acum, 0.0), axis=1, keepdims=True)
                    arow = act_s[pl.ds(r0 + ln, 1), :]
                    dtrow = dtt_s[pl.ds(r0 + ln, 1), :]
                    aend = jnp.sum(jnp.where(_lane_iota((1, t)) == end_lane, arow, 0.0), axis=1, keepdims=True)
                    decay_dt = jnp.exp2(jnp.where(tri, acol - (arow - jnp.log2(dtrow)), NEG_BIG))
                    m_heads.append((cb * decay_dt).astype(BF16))
                    bt_heads.append((bt * (dtrow * jnp.exp2(aend - arow))).astype(BF16))
                    acols.append(acol)
                    aends.append(aend)
                y = _dot(jnp.concatenate(m_heads, axis=1), x_stacked[p])
                y_tiles[p] = y if y_tiles[p] is None else y_tiles[p] + y
                sc_s[dirn, rows, lanes] = _dot(jnp.concatenate(bt_heads, axis=1), x_stacked[p])
                ea_s[dirn, rows, lanes] = jnp.exp2(jnp.where(half, acols[0], acols[1]))
                cd = jnp.exp2(jnp.where(half_row, aends[0], aends[1]))
                cd_s[dirn, pl.ds(pl.multiple_of(u * 8, 8), 8), lanes] = jnp.broadcast_to(cd, (8, LANES))
        for p, y in enumerate(y_tiles):
            y_s[rows, p * LANES:(p + 1) * LANES] = y
        return carry

    lax.fori_loop(0, n_chunks, local_chunk, 0, unroll=2)

    def carry_step(i, carry):
        u_fwd = i
        u_bwd = jnp.where(i < nc_c, nc_c - 1 - i, n_chunks + nc_c - 1 - i)
        for dirn, u in ((0, u_fwd), (1, u_bwd)):
            rows = pl.ds(pl.multiple_of(u * t, t), t)
            st = st_s[dirn]
            cd = cd_s[dirn, pl.ds(pl.multiple_of(u * 8, 8), 1), :]
            st_s[dirn] = st * cd + sc_s[dirn, rows, :]
            sc_s[dirn, rows, :] = st
        return carry

    st_s[...] = jnp.zeros_like(st_s)
    lax.fori_loop(0, n_chunks, carry_step, 0)

    def finish(z_ref, o_ref, n_chunks, u0):
        def body(c, carry):
            base = pl.multiple_of(c * t, t)
            rows = pl.ds(u0 + base, t)
            zz = z_ref[pl.ds(base, t), :]
            y = y_s[rows, :] + dch_ref[...] * xs_s[rows, :]
            for dirn in range(2):
                y = y + _dot(c_s[rows, :], sc_s[dirn, rows, :].astype(BF16)) * ea_s[dirn, rows, :]
            y = y * _silu(zz)
            ms = jnp.mean(y * y, axis=-1, keepdims=True)
            o_ref[pl.ds(base, t), :] = (y * lax.rsqrt(ms + RMS_EPS) * nw_ref[...]).astype(o_ref.dtype)
            return carry

        lax.fori_loop(0, n_chunks, body, 0, unroll=2)

    finish(z_c, oc_ref, nc_c, 0)
    finish(z_x, ox_ref, nc_x, n_c)


def _ssd(xbc_x, dt_x, z_x, xbc_c, dt_c, z_c, conv_w, conv_b, dt_bias, a_log, d_ch, norm_w, cos2, sin2, layer):
    bsz, n_x, _ = xbc_x.shape
    n_c = xbc_c.shape[1]
    n_all = n_x + n_c
    gw, st = GROUP_WIDTH, SSD_STATE
    b_blk0 = SSD_D_INNER // st
    c_blk0 = b_blk0 + SSD_GROUPS

    def seq_specs(n):
        return [
            pl.BlockSpec((None, n, LANES), lambda b, g: (b, 0, 2 * g)),
            pl.BlockSpec((None, n, LANES), lambda b, g: (b, 0, 2 * g + 1)),
            pl.BlockSpec((None, n, st), lambda b, g: (b, 0, b_blk0 + g)),
            pl.BlockSpec((None, n, st), lambda b, g: (b, 0, c_blk0 + g)),
            pl.BlockSpec((None, n, DT_PAD), lambda b, g: (b, 0, 0)),
            pl.BlockSpec((None, n, gw), lambda b, g: (b, 0, g)),
        ]

    in_specs = seq_specs(n_x) + seq_specs(n_c) + [
        pl.BlockSpec((None, 8, gw), lambda b, g: (layer, 0, g)),
        pl.BlockSpec((None, 8, st), lambda b, g: (layer, 0, b_blk0 + g)),
        pl.BlockSpec((None, 8, st), lambda b, g: (layer, 0, c_blk0 + g)),
        pl.BlockSpec((None, 1, gw), lambda b, g: (layer, 0, g)),
        pl.BlockSpec((None, 1, st), lambda b, g: (layer, 0, b_blk0 + g)),
        pl.BlockSpec((None, 1, st), lambda b, g: (layer, 0, c_blk0 + g)),
        pl.BlockSpec((None, 1, DT_PAD), lambda b, g: (layer, 0, 0)),
        pl.BlockSpec((None, 1, DT_PAD), lambda b, g: (layer, 0, 0)),
        pl.BlockSpec((None, 1, gw), lambda b, g: (layer, 0, g)),
        pl.BlockSpec((None, 1, gw), lambda b, g: (layer, 0, g)),
        _resident((n_x, st)),
        _resident((n_x, st)),
    ]
    out_shape = [jax.ShapeDtypeStruct((bsz, n_x, SSD_D_INNER), BF16),
                 jax.ShapeDtypeStruct((bsz, n_c, SSD_D_INNER), BF16)]
    out_specs = [pl.BlockSpec((None, n_x, gw), lambda b, g: (b, 0, g)),
                 pl.BlockSpec((None, n_c, gw), lambda b, g: (b, 0, g))]
    scratch = [
        pltpu.VMEM((n_all, gw), F32),
        pltpu.VMEM((n_all, st), BF16),
        pltpu.VMEM((n_all, st), F32),
        pltpu.VMEM((n_all, st), BF16),
        pltpu.VMEM((n_all, DT_PAD), F32),
        pltpu.VMEM((n_all, DT_PAD), F32),
        pltpu.VMEM((n_all, DT_PAD), F32),
        pltpu.VMEM((n_all, gw), F32),
        pltpu.VMEM((2, n_all, gw), F32),
        pltpu.VMEM((2, n_all, gw), F32),
        pltpu.VMEM((2, 8 * (n_all // SSD_CHUNK), gw), F32),
        pltpu.VMEM((2, st, gw), F32),
    ]
    return pl.pallas_call(
        _ssd_kernel,
        out_shape=out_shape,
        grid=(bsz, SSD_GROUPS),
        in_specs=in_specs,
        out_specs=out_specs,
        scratch_shapes=scratch,
        compiler_params=_params(52, 2),
        name="ssd_mixer",
    )(xbc_x, xbc_x, xbc_x, xbc_x, dt_x, z_x, xbc_c, xbc_c, xbc_c, xbc_c, dt_c, z_c,
      conv_w, conv_w, conv_w, conv_b, conv_b, conv_b, dt_bias, a_log, d_ch, norm_w, cos2, sin2)


NA_PAIR = 2
NA_DR = 2 * NA_WIN_ROWS


def _dot_tn(a, b):
    return lax.dot_general(a, b, (((0,), (0,)), ((), ())), preferred_element_type=F32)


def _pair_block_diag(q):
    lane = _lane_iota(q.shape)
    zero = jnp.zeros_like(q)
    return jnp.concatenate([jnp.where(lane < HALF, q, zero), jnp.where(lane >= HALF, q, zero)], axis=0)


def _pair_softmax_pv(scores, values, m_q):
    top = functools.reduce(jnp.maximum, [jnp.max(s, axis=0, keepdims=True) for s in scores])
    ps = [jnp.exp(s - top) for s in scores]
    inv = 1.0 / functools.reduce(jnp.add, [jnp.sum(p, axis=0, keepdims=True) for p in ps])
    out = functools.reduce(jnp.add, [_dot_tn(p.astype(BF16), v) for p, v in zip(ps, values)])
    inv_rows = jnp.concatenate(
        [jnp.broadcast_to(inv[:, c:c + LANES], (LANES, LANES)).T for c in range(0, 2 * m_q, LANES)], axis=0)
    out = out * inv_rows
    return jnp.where(_lane_iota((m_q, LANES)) < HALF, out[:m_q], out[m_q:])


def _na_kernel(q_ref, k_ref, v_ref, qc_ref, kc_ref, vc_ref, bias_ref, ox_ref, oc_ref, s_even, s_odd):
    n = q_ref.shape[0]
    rows = n // GRID_W
    win = NA_WIN_ROWS * GRID_W
    n_c = kc_ref.shape[0]

    def window_start(r):
        rs = jnp.clip(r - NA_WIN_ROWS // 2, 0, rows - NA_WIN_ROWS)
        return rs, pl.multiple_of(rs * GRID_W, GRID_W)

    def score_row(r, s_ref):
        rs, kstart = window_start(r)
        q_bd = _pair_block_diag(q_ref[pl.ds(pl.multiple_of(r * GRID_W, GRID_W), GRID_W), :])
        dr0 = rs - r + NA_WIN_ROWS - 1
        bias = jnp.concatenate([bias_ref[dr0 + 2 * i] for i in range(NA_WIN_ROWS // 2)], axis=0)
        s_ref[0:win, :] = _dot_nt(k_ref[pl.ds(kstart, win), :], q_bd) + bias
        s_ref[win:win + n_c, :] = _dot_nt(kc_ref[...], q_bd)

    def finish_row(r, s_ref):
        _, kstart = window_start(r)
        out = _pair_softmax_pv([s_ref[0:win, :], s_ref[win:win + n_c, :]],
                               [v_ref[pl.ds(kstart, win), :], vc_ref[...]], GRID_W)
        ox_ref[pl.ds(pl.multiple_of(r * GRID_W, GRID_W), GRID_W), :] = out.astype(ox_ref.dtype)

    score_row(0, s_even)

    def row_pair(i, carry):
        r = 2 * i
        score_row(r + 1, s_odd)
        finish_row(r, s_even)
        score_row(jnp.minimum(r + 2, rows - 1), s_even)
        finish_row(r + 1, s_odd)
        return carry

    lax.fori_loop(0, rows // 2, row_pair, 0)

    q_bd = _pair_block_diag(qc_ref[...])
    oc_ref[...] = _pair_softmax_pv([_dot_nt(kc_ref[...], q_bd)], [vc_ref[...]], n_c).astype(oc_ref.dtype)


def _na(qkv_x, qkv_c, bias_tab, layer):
    bsz, n_x, _ = qkv_x.shape
    n_c = qkv_c.shape[1]
    n_pairs = NA_HEADS // NA_PAIR
    blk = lambda n, off: pl.BlockSpec((None, n, LANES), lambda b, p: (b, 0, off + p))
    return pl.pallas_call(
        _na_kernel,
        out_shape=[jax.ShapeDtypeStruct((bsz, n_x, NA_WIDTH), BF16), jax.ShapeDtypeStruct((bsz, n_c, NA_WIDTH), BF16)],
        grid=(bsz, n_pairs),
        in_specs=[blk(n_x, 0), blk(n_x, n_pairs), blk(n_x, 2 * n_pairs),
                  blk(n_c, 0), blk(n_c, n_pairs), blk(n_c, 2 * n_pairs),
                  pl.BlockSpec((None, None, NA_DR, LANES, LANES), lambda b, p: (layer, p, 0, 0, 0))],
        out_specs=[blk(n_x, 0), blk(n_c, 0)],
        scratch_shapes=[pltpu.VMEM((NA_WIN_ROWS * GRID_W + n_c, NA_PAIR * GRID_W), F32)] * 2,
        compiler_params=_params(32, 2),
        name="na_mixer",
    )(qkv_x, qkv_x, qkv_x, qkv_c, qkv_c, qkv_c, bias_tab)


def _na_bias_table(na_rpb):
    depth, heads = na_rpb.shape[:2]
    kc = jnp.arange(GRID_W)[:, None]
    j = jnp.arange(GRID_W)[None, :]
    start = jnp.clip(j - NA_WIN_COLS // 2, 0, GRID_W - NA_WIN_COLS)
    valid = (kc >= start) & (kc < start + NA_WIN_COLS)
    dc = jnp.clip(kc - j + NA_WIN_COLS - 1, 0, 2 * NA_WIN_COLS - 2)
    tab = jnp.where(valid[None, None, None], na_rpb[:, :, :, dc], NEG_BIG)
    tab = jnp.pad(tab, ((0, 0), (0, 0), (0, NA_DR + 1 - tab.shape[2]), (0, 0), (0, 0)))
    tab = jnp.stack([tab[:, :, :NA_DR], tab[:, :, 1:NA_DR + 1]], axis=3)
    tab = tab.reshape(depth, heads // NA_PAIR, NA_PAIR, NA_DR, 2, GRID_W, GRID_W)
    tab = tab.transpose(0, 1, 3, 4, 5, 2, 6)
    return tab.reshape(depth, heads // NA_PAIR, NA_DR, 2 * GRID_W, NA_PAIR * GRID_W).astype(F32)


POOL_TILES = POOL_WIDTH // LANES


def _pool_kernel(*refs):
    u_tiles, (w_ref, scale_ref, o_ref) = refs[:POOL_TILES], refs[POOL_TILES:]
    t = SSD_CHUNK
    n = o_ref.shape[0]
    n_chunks = n // t

    def tile_windows(p):
        lo_g, hi_g = (p * LANES) // POOL_GROUP, ((p + 1) * LANES - 1) // POOL_GROUP
        split = (hi_g * POOL_GROUP) - p * LANES if hi_g != lo_g else None
        return POOL_WINDOWS[lo_g], POOL_WINDOWS[hi_g], split

    def chunk(base, first, last):
        pos = base + _row_iota((t, 1))
        tiles = []
        for p, u_ref in enumerate(u_tiles):
            w_lo, w_hi, split = tile_windows(p)
            x0 = u_ref[pl.ds(base, t), :]
            sums, acc, have = {}, x0, 1
            for w in POOL_WINDOWS:
                if w > w_hi:
                    break
                for d in list(range(-(w // 2), -(have // 2))) + list(range(have - have // 2, w // 2)):
                    acc = acc + _shifted_rows(u_ref, base, d, first, last)
                have = w
                cnt = (jnp.minimum(pos + w // 2, n) - jnp.maximum(pos - w // 2, 0)).astype(F32)
                sums[w] = acc / cnt
            if split is None:
                pooled = sums[w_lo]
            else:
                pooled = jnp.where(_lane_iota((t, LANES)) < split, sums[w_lo], sums[w_hi])
            tiles.append((pooled - x0).astype(BF16))
        mapped = _dot(jnp.concatenate(tiles, axis=1), w_ref[...])
        o_ref[pl.ds(base, t), :] = (mapped * scale_ref[...]).astype(o_ref.dtype)

    chunk(0, True, False)

    def body(c, carry):
        chunk(pl.multiple_of(c * t, t), False, False)
        return carry

    lax.fori_loop(1, n_chunks - 1, body, 0)
    chunk((n_chunks - 1) * t, False, True)


def _pool(u, w_bd, scale, layer):
    bsz, n, _ = u.shape
    return pl.pallas_call(
        _pool_kernel,
        out_shape=jax.ShapeDtypeStruct((bsz, n, POOL_WIDTH), BF16),
        grid=(bsz,),
        in_specs=[pl.BlockSpec((None, n, LANES), lambda b, p=p: (b, 0, p)) for p in range(POOL_TILES)]
        + [_layer_resident((POOL_WIDTH, POOL_WIDTH), layer), _layer_resident((1, POOL_WIDTH), layer)],
        out_specs=pl.BlockSpec((None, n, POOL_WIDTH), lambda b: (b, 0, 0)),
        compiler_params=_params(40),
        name="pool_mixer",
    )(*([u] * POOL_TILES), w_bd, scale)


def _fnet_kernel(u_ref, cc_ref, sc_ref, pos_ref, o_ref, z_s):
    n = u_ref.shape[0]
    rows = 256

    @pl.when(pl.program_id(1) == 0)
    def _():
        for r in range(n // rows):
            ub = u_ref[r * rows:(r + 1) * rows, :].astype(BF16)
            z_s[r * rows:(r + 1) * rows, :] = _dot(ub, cc_ref[...]).astype(BF16)
            z_s[n + r * rows:n + (r + 1) * rows, :] = _dot(ub, sc_ref[...]).astype(BF16)

    scale = 1.0 / math.sqrt(n * FNET_HEAD_DIM)
    o_ref[...] = (_dot(pos_ref[...], z_s[...]) * scale).astype(o_ref.dtype)


def _fnet(u, chan_cos, chan_sin, pos_tab, tq):
    bsz, n, _ = u.shape
    return pl.pallas_call(
        _fnet_kernel,
        out_shape=jax.ShapeDtypeStruct((bsz, n, FNET_WIDTH), BF16),
        grid=(bsz, n // tq),
        in_specs=[pl.BlockSpec((None, n, FNET_WIDTH), lambda b, j: (b, 0, 0)),
                  _resident((FNET_WIDTH, FNET_WIDTH)),
                  _resident((FNET_WIDTH, FNET_WIDTH)),
                  pl.BlockSpec((tq, 2 * n), lambda b, j: (j, 0))],
        out_specs=pl.BlockSpec((None, tq, FNET_WIDTH), lambda b, j: (b, j, 0)),
        scratch_shapes=[pltpu.VMEM((2 * n, FNET_WIDTH), BF16)],
        compiler_params=_params(48, 2),
        name="fnet_mixer",
    )(u, chan_cos, chan_sin, pos_tab)


def _dft_cos_sin(n):
    k = jnp.arange(n, dtype=jnp.int32)
    ang = ((k[:, None] * k[None, :]) % n).astype(F32) * (2.0 * math.pi / n)
    return jnp.cos(ang), jnp.sin(ang)


def _fnet_tables(n):
    c, s = _dft_cos_sin(n)
    return jnp.concatenate([c, -s], axis=1).astype(BF16)


def _fnet_channel_tables():
    c, s = _dft_cos_sin(FNET_HEAD_DIM)
    eye = jnp.eye(FNET_HEADS, dtype=F32)
    return jnp.kron(eye, c).astype(BF16), jnp.kron(eye, s).astype(BF16)


def _merge_kernel(x_ref, mod_ref, ssd_ref, na_ref, pool_ref, fnet_ref,
                  wg_ref, bg_ref, wssd_ref, wna_ref, wpool_ref, wfnet_ref, wmix_ref, g_ref, b_ref, o_ref):
    x = x_ref[...]
    h = (_ln(x) * (1.0 + mod_ref[1:2, :]) + mod_ref[0:1, :]).astype(BF16)
    merged = None
    branches = ((ssd_ref, wssd_ref), (na_ref, wna_ref), (pool_ref, wpool_ref), (fnet_ref, wfnet_ref))
    for i, (y_ref, w_ref) in enumerate(branches):
        cols = slice(i * D_MODEL, (i + 1) * D_MODEL)
        gate = jax.nn.sigmoid(_dot(h, wg_ref[:, cols]) + bg_ref[:, cols])
        term = gate * _dot(y_ref[...], w_ref[...])
        merged = term if merged is None else merged + term
    mix = _dot(merged.astype(BF16), wmix_ref[...])
    y = _ln(DEEPNORM_ALPHA * x + mod_ref[2:3, :] * mix)
    o_ref[...] = y * g_ref[...] + b_ref[...]


def _merge(x2d, mod, tiles_per_mod, layer, ssd, na, pool, fnet, weights, tm):
    m = x2d.shape[0]
    mod_map = (lambda i: (0, 0, 0)) if tiles_per_mod is None else (lambda i: (i // tiles_per_mod, 0, 0))
    tok = lambda n: pl.BlockSpec((tm, n), lambda i: (i, 0))
    wg, bg, wssd, wna, wpool, wfnet, wmix, ln_g, ln_b = weights
    in_specs = [tok(D_MODEL), pl.BlockSpec((None, 8, D_MODEL), mod_map),
                tok(SSD_D_INNER), tok(NA_WIDTH), tok(POOL_WIDTH), tok(FNET_WIDTH),
                _layer_resident((D_MODEL, N_BRANCH * D_MODEL), layer),
                _layer_resident((1, N_BRANCH * D_MODEL), layer),
                _layer_resident((SSD_D_INNER, D_MODEL), layer),
                _layer_resident((NA_WIDTH, D_MODEL), layer),
                _layer_resident((POOL_WIDTH, D_MODEL), layer),
                _layer_resident((FNET_WIDTH, D_MODEL), layer),
                _layer_resident((D_MODEL, D_MODEL), layer),
                _layer_resident((1, D_MODEL), layer),
                _layer_resident((1, D_MODEL), layer)]
    return pl.pallas_call(
        _merge_kernel,
        out_shape=jax.ShapeDtypeStruct((m, D_MODEL), F32),
        grid=(m // tm,),
        in_specs=in_specs,
        out_specs=tok(D_MODEL),
        compiler_params=_params(52),
        name="merge_mix",
    )(x2d, mod, ssd, na, pool, fnet, wg, bg, wssd, wna, wpool, wfnet, wmix, ln_g, ln_b)


MLP_CHUNK = 1024


def _mlp_kernel(x_ref, mod_ref, wup_ref, bup_ref, wdn_ref, bdn_ref, g_ref, b_ref, o_ref):
    x = x_ref[...]
    h = (_ln(x) * (1.0 + mod_ref[4:5, :]) + mod_ref[3:4, :]).astype(BF16)
    acc = None
    for c in range(D_FF // MLP_CHUNK):
        cols = slice(c * MLP_CHUNK, (c + 1) * MLP_CHUNK)
        a = jnp.maximum(_dot(h, wup_ref[:, cols]) + bup_ref[:, cols], 0.0)
        part = _dot((a * a).astype(BF16), wdn_ref[cols, :])
        acc = part if acc is None else acc + part
    y = _ln(DEEPNORM_ALPHA * x + mod_ref[5:6, :] * (acc + bdn_ref[...]))
    o_ref[...] = y * g_ref[...] + b_ref[...]


def _mlp(x2d, mod, tiles_per_mod, layer, weights, tm):
    m = x2d.shape[0]
    mod_map = (lambda i: (0, 0, 0)) if tiles_per_mod is None else (lambda i: (i // tiles_per_mod, 0, 0))
    tok = pl.BlockSpec((tm, D_MODEL), lambda i: (i, 0))
    wup, bup, wdn, bdn, ln_g, ln_b = weights
    in_specs = [tok, pl.BlockSpec((None, 8, D_MODEL), mod_map),
                _layer_resident((D_MODEL, D_FF), layer), _layer_resident((1, D_FF), layer),
                _layer_resident((D_FF, D_MODEL), layer), _layer_resident((1, D_MODEL), layer),
                _layer_resident((1, D_MODEL), layer), _layer_resident((1, D_MODEL), layer)]
    return pl.pallas_call(
        _mlp_kernel,
        out_shape=jax.ShapeDtypeStruct((m, D_MODEL), F32),
        grid=(m // tm,),
        in_specs=in_specs,
        out_specs=tok,
        compiler_params=_params(52),
        name="mlp",
    )(x2d, mod, wup, bup, wdn, bdn, ln_g, ln_b)


def _rope_tables(n):
    pos = jnp.arange(n)
    quarter = SSD_STATE // 4
    inv_freq = ROPE_THETA ** (-jnp.arange(quarter, dtype=F32) / quarter)
    row = (pos // GRID_W).astype(F32)[:, None] * inv_freq
    col = (pos % GRID_W).astype(F32)[:, None] * inv_freq
    ang = jnp.concatenate([row, col], -1)
    cos, sin = jnp.cos(ang), jnp.sin(ang)
    return jnp.concatenate([cos, cos], -1), jnp.concatenate([-sin, sin], -1)


def _dt_lane_layout(a):
    lead = a.shape[:-2]
    a = a.reshape(lead + (2, SSD_GROUPS, HEADS_PER_GROUP))
    a = jnp.swapaxes(a, -3, -2).reshape(lead + (2 * SSD_HEADS,))
    return jnp.pad(a, [(0, 0)] * len(lead) + [(0, DT_PAD - 2 * SSD_HEADS)])


def _block_diag(w):
    l, g, c, _ = w.shape
    eye = jnp.eye(g, dtype=w.dtype)
    return jnp.einsum("lgcd,gh->lgchd", w, eye).reshape(l, g * c, g * c)


def kernel(x, c, ctx, c_ctx, ada_w, ada_b, in_w, in_b, ssd_conv_w, ssd_conv_b, ssd_dt_bias, ssd_a_log, ssd_d,
           ssd_norm_w, ssd_out_w, na_rpb, na_out_w, pool_w, pool_scale, pool_out_w, fnet_out_w, mix_out_w,
           ln1_g, ln1_b, mlp_up_w, mlp_up_b, mlp_down_w, mlp_down_b, ln2_g, ln2_b):
    bsz, n_x, d = x.shape
    n_c = ctx.shape[1]
    depth = in_w.shape[0]

    offs = [0]
    for wdt in (SSD_D_INNER, SSD_XBC, 2 * SSD_HEADS, 3 * NA_WIDTH, POOL_WIDTH, FNET_WIDTH, N_BRANCH * D_MODEL):
        offs.append(offs[-1] + wdt)

    def in_cols(a, i):
        return a[..., offs[i]:offs[i + 1]]

    def split_in(a):
        dt_cols = in_cols(a, 2)
        dt_cols = _dt_lane_layout(dt_cols.reshape(dt_cols.shape[:-1] + (2, SSD_HEADS)))
        qkv = in_cols(a, 3)
        q_scale = jnp.concatenate([jnp.full((NA_WIDTH,), NA_HEAD_DIM ** -0.5, F32), jnp.ones((2 * NA_WIDTH,), F32)])
        return [in_cols(a, 0), in_cols(a, 1), dt_cols, qkv * q_scale, in_cols(a, 4), in_cols(a, 5)], in_cols(a, 6)

    w_parts, w_gate = split_in(in_w)
    b_parts, b_gate = split_in(in_b[:, None, :])
    w_parts = [w.astype(BF16) for w in w_parts]
    w_gate = w_gate.astype(BF16)

    conv_w = jnp.pad(ssd_conv_w, ((0, 0), (0, 8 - SSD_CONV), (0, 0)))
    conv_b = ssd_conv_b[:, None, :]
    dt_bias = _dt_lane_layout(ssd_dt_bias)[:, None, :]
    a_log = _dt_lane_layout(ssd_a_log)[:, None, :]
    d_ch = jnp.repeat(ssd_d, SSD_HEADDIM, axis=-1)[:, None, :]
    norm_w = ssd_norm_w[:, None, :]
    cos2, sin2 = _rope_tables(n_x)

    bias_tab = _na_bias_table(na_rpb)
    pool_bd = _block_diag(pool_w).astype(BF16)
    pool_sc = pool_scale[:, None, :]
    chan_cos, chan_sin = _fnet_channel_tables()
    pos_tab_x = _fnet_tables(n_x)
    pos_tab_c = _fnet_tables(n_c)

    merge_w = [w_gate, b_gate, ssd_out_w.astype(BF16), na_out_w.astype(BF16), pool_out_w.astype(BF16),
               fnet_out_w.astype(BF16), mix_out_w.astype(BF16), ln1_g[:, None, :], ln1_b[:, None, :]]
    mlp_w = [mlp_up_w.astype(BF16), mlp_up_b[:, None, :], mlp_down_w.astype(BF16), mlp_down_b[:, None, :],
             ln2_g[:, None, :], ln2_b[:, None, :]]

    rows = ((bsz + 1 + 7) // 8) * 8
    c_all = jnp.zeros((rows, d), F32).at[:bsz].set(c).at[bsz].set(c_ctx)
    mods = _ada_all_layers(c_all, ada_w, ada_b).reshape(depth, rows, 6, d)
    mods = jnp.pad(mods, ((0, 0), (0, 0), (0, 2), (0, 0)))

    tm = 256
    tiles_x = n_x // tm
    xf = x.reshape(bsz * n_x, d)
    cf = ctx.reshape(bsz * n_c, d)
    for i in range(depth):
        need_ctx = i < depth - 1
        mod_x, mod_c = mods[i, :bsz], mods[i, bsz:bsz + 1]
        z_x, xbc_x, dt_x, qkv_x, pool_x, fnet_x = _inproj(xf, mod_x, tiles_x, i, w_parts, b_parts, tm)
        z_c, xbc_c, dt_c, qkv_c, pool_c, fnet_c = _inproj(cf, mod_c, None, i, w_parts, b_parts, tm)
        r3 = lambda a, n: a.reshape(bsz, n, a.shape[-1])
        ssd_x, ssd_c = _ssd(r3(xbc_x, n_x), r3(dt_x, n_x), r3(z_x, n_x), r3(xbc_c, n_c), r3(dt_c, n_c), r3(z_c, n_c),
                            conv_w, conv_b, dt_bias, a_log, d_ch, norm_w, cos2, sin2, i)
        na_x, na_c = _na(r3(qkv_x, n_x), r3(qkv_c, n_c), bias_tab, i)
        pm_x = _pool(r3(pool_x, n_x), pool_bd, pool_sc, i)
        fm_x = _fnet(r3(fnet_x, n_x), chan_cos, chan_sin, pos_tab_x, 512)
        f2 = lambda a: a.reshape(-1, a.shape[-1])
        x1 = _merge(xf, mod_x, tiles_x, i, f2(ssd_x), f2(na_x), f2(pm_x), f2(fm_x), merge_w, tm)
        xf = _mlp(x1, mod_x, tiles_x, i, mlp_w, tm)
        if need_ctx:
            pm_c = _pool(r3(pool_c, n_c), pool_bd, pool_sc, i)
            fm_c = _fnet(r3(fnet_c, n_c), chan_cos, chan_sin, pos_tab_c, n_c)
            c1 = _merge(cf, mod_c, None, i, f2(ssd_c), f2(na_c), f2(pm_c), f2(fm_c), merge_w, tm)
            cf = _mlp(c1, mod_c, None, i, mlp_w, tm)
    return xf.reshape(bsz, n_x, d)
```

```python
import functools
import math

import jax
import jax.numpy as jnp
from jax import lax
from jax.experimental import pallas as pl
from jax.experimental.pallas import tpu as pltpu

F32 = jnp.float32
BF16 = jnp.bfloat16

D_MODEL = 1024
DEPTH = 4
GRID_W = 64

SSD_HEADS = 16
SSD_HEADDIM = 64
SSD_D_INNER = SSD_HEADS * SSD_HEADDIM
SSD_GROUPS = 4
SSD_STATE = 128
SSD_CONV = 5
SSD_CHUNK = 128
SSD_XBC = SSD_D_INNER + 2 * SSD_GROUPS * SSD_STATE
HEADS_PER_GROUP = SSD_HEADS // SSD_GROUPS
GROUP_WIDTH = HEADS_PER_GROUP * SSD_HEADDIM

NA_HEADS = 16
NA_HEAD_DIM = 64
NA_WIDTH = NA_HEADS * NA_HEAD_DIM
NA_WIN_ROWS = 8
NA_WIN_COLS = 16

POOL_WINDOWS = (2, 4, 8, 16)
POOL_GROUP = 192
POOL_WIDTH = 4 * POOL_GROUP

FNET_HEADS = 4
FNET_HEAD_DIM = 192
FNET_WIDTH = FNET_HEADS * FNET_HEAD_DIM

N_BRANCH = 4
D_FF = 4 * D_MODEL
ROPE_THETA = 10000.0
DEEPNORM_ALPHA = (2 * DEPTH) ** 0.25
LN_EPS = 1e-6
RMS_EPS = 1e-5

LANES = 128
HALF = LANES // 2
DT_PAD = LANES
TOKEN_TILE = 512
NEG_BIG = -1e30
LOG2_E = 1.4426950408889634
MIB = 1024 * 1024


def _resident(shape):
    nd = len(shape)
    return pl.BlockSpec(shape, lambda *_: (0,) * nd, pipeline_mode=pl.Buffered(1))


def _layer_resident(shape, layer):
    nd = len(shape)
    return pl.BlockSpec((None,) + tuple(shape), lambda *_: (layer,) + (0,) * nd, pipeline_mode=pl.Buffered(1))


def _params(vmem_mib, n_axes=1):
    return pltpu.CompilerParams(dimension_semantics=("arbitrary",) * n_axes, vmem_limit_bytes=vmem_mib * MIB)


def _ln(x):
    mu = jnp.mean(x, axis=-1, keepdims=True)
    xc = x - mu
    var = jnp.mean(xc * xc, axis=-1, keepdims=True)
    return xc * lax.rsqrt(var + LN_EPS)


def _silu(x):
    return x * jax.nn.sigmoid(x)


def _dot(a, b):
    return jnp.dot(a, b, preferred_element_type=F32)


def _dot_nt(a, b):
    return lax.dot_general(a, b, (((1,), (1,)), ((), ())), preferred_element_type=F32)


def _ada_kernel(c_ref, w_ref, b_ref, o_ref):
    a = _silu(c_ref[...]).astype(BF16)
    o_ref[...] = _dot(a, w_ref[...].astype(BF16)) + b_ref[...]


def _ada_all_layers(c_all, ada_w, ada_b):
    rows = c_all.shape[0]
    return pl.pallas_call(
        _ada_kernel,
        out_shape=jax.ShapeDtypeStruct((DEPTH, rows, 6 * D_MODEL), F32),
        grid=(DEPTH, 6),
        in_specs=[
            _resident((rows, D_MODEL)),
            pl.BlockSpec((None, D_MODEL, D_MODEL), lambda l, j: (l, 0, j)),
            pl.BlockSpec((None, 1, D_MODEL), lambda l, j: (l, 0, j)),
        ],
        out_specs=pl.BlockSpec((None, rows, D_MODEL), lambda l, j: (l, 0, j)),
        compiler_params=_params(32, 2),
        name="ada_mod",
    )(c_all, ada_w, ada_b.reshape(DEPTH, 1, 6 * D_MODEL))


IN_SPLITS = (SSD_D_INNER, SSD_XBC, DT_PAD, 3 * NA_WIDTH, POOL_WIDTH, FNET_WIDTH)
IN_DTYPES = (F32, F32, F32, BF16, F32, F32)


def _inproj_kernel(x_ref, mod_ref, *refs):
    n = len(IN_SPLITS)
    w_refs, b_refs, o_refs = refs[:n], refs[n:2 * n], refs[2 * n:]
    h = (_ln(x_ref[...]) * (1.0 + mod_ref[1:2, :]) + mod_ref[0:1, :]).astype(BF16)
    for w_ref, b_ref, o_ref in zip(w_refs, b_refs, o_refs):
        o_ref[...] = (_dot(h, w_ref[...]) + b_ref[...]).astype(o_ref.dtype)


def _inproj(x2d, mod, tiles_per_mod, layer, weights, biases, tm):
    m = x2d.shape[0]
    if tiles_per_mod is None:
        mod_map = lambda i: (0, 0, 0)
    else:
        mod_map = lambda i: (i // tiles_per_mod, 0, 0)
    in_specs = [pl.BlockSpec((tm, D_MODEL), lambda i: (i, 0)), pl.BlockSpec((None, 8, D_MODEL), mod_map)]
    in_specs += [_layer_resident((D_MODEL, n), layer) for n in IN_SPLITS]
    in_specs += [_layer_resident((1, n), layer) for n in IN_SPLITS]
    return pl.pallas_call(
        _inproj_kernel,
        out_shape=[jax.ShapeDtypeStruct((m, n), dt) for n, dt in zip(IN_SPLITS, IN_DTYPES)],
        grid=(m // tm,),
        in_specs=in_specs,
        out_specs=[pl.BlockSpec((tm, n), lambda i: (i, 0)) for n in IN_SPLITS],
        compiler_params=_params(58),
        name="in_proj",
    )(x2d, mod, *weights, *biases)


def _row_iota(shape):
    return lax.broadcasted_iota(jnp.int32, shape, 0)


def _lane_iota(shape):
    return lax.broadcasted_iota(jnp.int32, shape, 1)


def _shifted_rows(ref, base, d, first, last):
    t = SSD_CHUNK
    if d == 0 or not ((first and d < 0) or (last and d > 0)):
        return ref[pl.ds(base + d, t), :]
    x = ref[pl.ds(base, t), :]
    row = _row_iota(x.shape)
    if d < 0:
        return jnp.where(row >= -d, pltpu.roll(x, -d, 0), 0.0)
    return jnp.where(row < t - d, pltpu.roll(x, t - d, 0), 0.0)


def _split3(x):
    hi = x.astype(BF16)
    r1 = x - hi.astype(F32)
    mid = r1.astype(BF16)
    lo = (r1 - mid.astype(F32)).astype(BF16)
    return hi, mid, lo


def _ssd_kernel(xs0_x, xs1_x, bm_x, cm_x, dt_x, z_x, xs0_c, xs1_c, bm_c, cm_c, dt_c, z_c,
                wxs, wbm, wcm, bxs, bbm, bcm, dtb_ref, alog_ref, dch_ref, nw_ref, cos_ref, sin_ref,
                ox_ref, oc_ref,
                xs_s, b_s, bt_s, c_s, dtt_s, ac_s, act_s, y_s, ea_s, sc_s, cd_s, st_s):
    t = SSD_CHUNK
    n_x, n_c = bm_x.shape[0], bm_c.shape[0]
    nc_x, nc_c = n_x // t, n_c // t
    g = pl.program_id(1)
    half = _lane_iota((t, LANES)) < HALF
    half_row = _lane_iota((1, LANES)) < HALF
    li, si = _row_iota((t, t)), _lane_iota((t, t))
    tril = jnp.where(si <= li, 1.0, 0.0).astype(BF16)
    triu = jnp.where(si >= li, 1.0, 0.0).astype(BF16)
    is_fwd_lane = (_lane_iota((t, LANES)) % (2 * HEADS_PER_GROUP)) < HEADS_PER_GROUP
    a_row = -jnp.exp(alog_ref[...])

    def conv_silu(ref, w_ref, b_ref, base, lanes, first, last):
        acc = b_ref[:, lanes] + w_ref[0:1, lanes] * _shifted_rows(ref, base, -(SSD_CONV // 2), first, last)
        for k in range(1, SSD_CONV):
            acc = acc + w_ref[k:k + 1, lanes] * _shifted_rows(ref, base, k - SSD_CONV // 2, first, last)
        return _silu(acc)

    def prep_chunk(refs, base, u0, first, last, rope):
        xs_tiles, bm_r, cm_r = refs
        dst = pl.ds(u0 + base, t)
        for p, xs_r in enumerate(xs_tiles):
            lanes = slice(p * LANES, (p + 1) * LANES)
            xs_s[dst, lanes] = conv_silu(xs_r, wxs, bxs, base, lanes, first, last)
        bm = conv_silu(bm_r, wbm, bbm, base, slice(None), first, last)
        cm = conv_silu(cm_r, wcm, bcm, base, slice(None), first, last)
        if rope:
            cos2, sin2 = cos_ref[pl.ds(base, t), :], sin_ref[pl.ds(base, t), :]
            bm = bm * cos2 + pltpu.roll(bm, HALF, 1) * sin2
            cm = cm * cos2 + pltpu.roll(cm, HALF, 1) * sin2
        b_s[dst, :] = bm.astype(BF16)
        bt_s[dst, :] = bm.T
        c_s[dst, :] = cm.astype(BF16)

    def prep_dt(dt_r, n_chunks, u0):
        def body(c, carry):
            base = pl.multiple_of(c * t, t)
            dst = pl.ds(u0 + base, t)
            raw = dt_r[pl.ds(base, t), :] + dtb_ref[...]
            dt = jnp.maximum(raw, 0.0) + jnp.log1p(jnp.exp(-jnp.abs(raw)))
            dtt_s[dst, :] = dt.T
            parts = _split3(dt * a_row)
            prefix = _dot(tril, parts[0]) + _dot(tril, parts[1]) + _dot(tril, parts[2])
            suffix = _dot(triu, parts[0]) + _dot(triu, parts[1]) + _dot(triu, parts[2])
            acum = jnp.where(is_fwd_lane, prefix, suffix) * LOG2_E
            ac_s[dst, :] = acum
            act_s[dst, :] = acum.T
            return carry

        lax.fori_loop(0, n_chunks, body, 0, unroll=2)

    @pl.when(g == 0)
    def _():
        prep_dt(dt_c, nc_c, 0)
        prep_dt(dt_x, nc_x, n_c)

    def prep_seq(refs, n_chunks, u0, rope):
        prep_chunk(refs, 0, u0, True, False, rope)

        def body(c, carry):
            prep_chunk(refs, pl.multiple_of(c * t, t), u0, False, False, rope)
            return carry

        lax.fori_loop(1, n_chunks - 1, body, 0)
        prep_chunk(refs, (n_chunks - 1) * t, u0, False, True, rope)

    prep_seq(((xs0_c, xs1_c), bm_c, cm_c), nc_c, 0, False)
    prep_seq(((xs0_x, xs1_x), bm_x, cm_x), nc_x, n_c, True)

    n_chunks = nc_c + nc_x

    def local_chunk(u, carry):
        r0 = pl.multiple_of(u * t, t)
        rows = pl.ds(r0, t)
        cb = _dot_nt(c_s[rows, :], b_s[rows, :])
        acum, bt = ac_s[rows, :], bt_s[rows, :]
        lane = _lane_iota((t, LANES))
        x_stacked = []
        for p in range(GROUP_WIDTH // LANES):
            xb = xs_s[rows, p * LANES:(p + 1) * LANES].astype(BF16)
            zero = jnp.zeros_like(xb)
            x_stacked.append(jnp.concatenate([jnp.where(half, xb, zero), jnp.where(half, zero, xb)], axis=0))
        y_tiles = [None] * (GROUP_WIDTH // LANES)
        for dirn in range(2):
            tri = (si <= li) if dirn == 0 else (si >= li)
            end_lane = t - 1 if dirn == 0 else 0
            for p in range(GROUP_WIDTH // LANES):
                lanes = slice(p * LANES, (p + 1) * LANES)
                acols, aends, m_heads, bt_heads = [], [], [], []
                for jj in range(2):
                    ln = g * (2 * HEADS_PER_GROUP) + dirn * HEADS_PER_GROUP + 2 * p + jj
                    acol = jnp.sum(jnp.where(lane == ln, acum, 0.0), axis=1, keepdims=True)
                    arow = act_s[pl.ds(r0 + ln, 1), :]
                    dtrow = dtt_s[pl.ds(r0 + ln, 1), :]
                    aend = jnp.sum(jnp.where(_lane_iota((1, t)) == end_lane, arow, 0.0), axis=1, keepdims=True)
                    decay_dt = jnp.exp2(jnp.where(tri, acol - (arow - jnp.log2(dtrow)), NEG_BIG))
                    m_heads.append((cb * decay_dt).astype(BF16))
                    bt_heads.append((bt * (dtrow * jnp.exp2(aend - arow))).astype(BF16))
                    acols.append(acol)
                    aends.append(aend)
                y = _dot(jnp.concatenate(m_heads, axis=1), x_stacked[p])
                y_tiles[p] = y if y_tiles[p] is None else y_tiles[p] + y
                sc_s[dirn, rows, lanes] = _dot(jnp.concatenate(bt_heads, axis=1), x_stacked[p])
                ea_s[dirn, rows, lanes] = jnp.exp2(jnp.where(half, acols[0], acols[1]))
                cd = jnp.exp2(jnp.where(half_row, aends[0], aends[1]))
                cd_s[dirn, pl.ds(pl.multiple_of(u * 8, 8), 8), lanes] = jnp.broadcast_to(cd, (8, LANES))
        for p, y in enumerate(y_tiles):
            y_s[rows, p * LANES:(p + 1) * LANES] = y
        return carry

    lax.fori_loop(0, n_chunks, local_chunk, 0, unroll=2)

    def carry_step(i, carry):
        u_fwd = i
        u_bwd = jnp.where(i < nc_c, nc_c - 1 - i, n_chunks + nc_c - 1 - i)
        for dirn, u in ((0, u_fwd), (1, u_bwd)):
            rows = pl.ds(pl.multiple_of(u * t, t), t)
            st = st_s[dirn]
            cd = cd_s[dirn, pl.ds(pl.multiple_of(u * 8, 8), 1), :]
            st_s[dirn] = st * cd + sc_s[dirn, rows, :]
            sc_s[dirn, rows, :] = st
        return carry

    st_s[...] = jnp.zeros_like(st_s)
    lax.fori_loop(0, n_chunks, carry_step, 0)

    def finish(z_ref, o_ref, n_chunks, u0):
        def body(c, carry):
            base = pl.multiple_of(c * t, t)
            rows = pl.ds(u0 + base, t)
            zz = z_ref[pl.ds(base, t), :]
            y = y_s[rows, :] + dch_ref[...] * xs_s[rows, :]
            for dirn in range(2):
                y = y + _dot(c_s[rows, :], sc_s[dirn, rows, :].astype(BF16)) * ea_s[dirn, rows, :]
            y = y * _silu(zz)
            ms = jnp.mean(y * y, axis=-1, keepdims=True)
            o_ref[pl.ds(base, t), :] = (y * lax.rsqrt(ms + RMS_EPS) * nw_ref[...]).astype(o_ref.dtype)
            return carry

        lax.fori_loop(0, n_chunks, body, 0, unroll=2)

    finish(z_c, oc_ref, nc_c, 0)
    finish(z_x, ox_ref, nc_x, n_c)


def _ssd(xbc_x, dt_x, z_x, xbc_c, dt_c, z_c, conv_w, conv_b, dt_bias, a_log, d_ch, norm_w, cos2, sin2, layer):
    bsz, n_x, _ = xbc_x.shape
    n_c = xbc_c.shape[1]
    n_all = n_x + n_c
    gw, st = GROUP_WIDTH, SSD_STATE
    b_blk0 = SSD_D_INNER // st
    c_blk0 = b_blk0 + SSD_GROUPS

    def seq_specs(n):
        return [
            pl.BlockSpec((None, n, LANES), lambda b, g: (b, 0, 2 * g)),
            pl.BlockSpec((None, n, LANES), lambda b, g: (b, 0, 2 * g + 1)),
            pl.BlockSpec((None, n, st), lambda b, g: (b, 0, b_blk0 + g)),
            pl.BlockSpec((None, n, st), lambda b, g: (b, 0, c_blk0 + g)),
            pl.BlockSpec((None, n, DT_PAD), lambda b, g: (b, 0, 0)),
            pl.BlockSpec((None, n, gw), lambda b, g: (b, 0, g)),
        ]

    in_specs = seq_specs(n_x) + seq_specs(n_c) + [
        pl.BlockSpec((None, 8, gw), lambda b, g: (layer, 0, g)),
        pl.BlockSpec((None, 8, st), lambda b, g: (layer, 0, b_blk0 + g)),
        pl.BlockSpec((None, 8, st), lambda b, g: (layer, 0, c_blk0 + g)),
        pl.BlockSpec((None, 1, gw), lambda b, g: (layer, 0, g)),
        pl.BlockSpec((None, 1, st), lambda b, g: (layer, 0, b_blk0 + g)),
        pl.BlockSpec((None, 1, st), lambda b, g: (layer, 0, c_blk0 + g)),
        pl.BlockSpec((None, 1, DT_PAD), lambda b, g: (layer, 0, 0)),
        pl.BlockSpec((None, 1, DT_PAD), lambda b, g: (layer, 0, 0)),
        pl.BlockSpec((None, 1, gw), lambda b, g: (layer, 0, g)),
        pl.BlockSpec((None, 1, gw), lambda b, g: (layer, 0, g)),
        _resident((n_x, st)),
        _resident((n_x, st)),
    ]
    out_shape = [jax.ShapeDtypeStruct((bsz, n_x, SSD_D_INNER), BF16),
                 jax.ShapeDtypeStruct((bsz, n_c, SSD_D_INNER), BF16)]
    out_specs = [pl.BlockSpec((None, n_x, gw), lambda b, g: (b, 0, g)),
                 pl.BlockSpec((None, n_c, gw), lambda b, g: (b, 0, g))]
    scratch = [
        pltpu.VMEM((n_all, gw), F32),
        pltpu.VMEM((n_all, st), BF16),
        pltpu.VMEM((n_all, st), F32),
        pltpu.VMEM((n_all, st), BF16),
        pltpu.VMEM((n_all, DT_PAD), F32),
        pltpu.VMEM((n_all, DT_PAD), F32),
        pltpu.VMEM((n_all, DT_PAD), F32),
        pltpu.VMEM((n_all, gw), F32),
        pltpu.VMEM((2, n_all, gw), F32),
        pltpu.VMEM((2, n_all, gw), F32),
        pltpu.VMEM((2, 8 * (n_all // SSD_CHUNK), gw), F32),
        pltpu.VMEM((2, st, gw), F32),
    ]
    return pl.pallas_call(
        _ssd_kernel,
        out_shape=out_shape,
        grid=(bsz, SSD_GROUPS),
        in_specs=in_specs,
        out_specs=out_specs,
        scratch_shapes=scratch,
        compiler_params=_params(52, 2),
        name="ssd_mixer",
    )(xbc_x, xbc_x, xbc_x, xbc_x, dt_x, z_x, xbc_c, xbc_c, xbc_c, xbc_c, dt_c, z_c,
      conv_w, conv_w, conv_w, conv_b, conv_b, conv_b, dt_bias, a_log, d_ch, norm_w, cos2, sin2)


NA_PAIR = 2
NA_DR = 2 * NA_WIN_ROWS


def _dot_tn(a, b):
    return lax.dot_general(a, b, (((0,), (0,)), ((), ())), preferred_element_type=F32)


def _pair_block_diag(q):
    lane = _lane_iota(q.shape)
    zero = jnp.zeros_like(q)
    return jnp.concatenate([jnp.where(lane < HALF, q, zero), jnp.where(lane >= HALF, q, zero)], axis=0)


def _pair_softmax_pv(scores, values, m_q):
    top = functools.reduce(jnp.maximum, [jnp.max(s, axis=0, keepdims=True) for s in scores])
    ps = [jnp.exp(s - top) for s in scores]
    inv = 1.0 / functools.reduce(jnp.add, [jnp.sum(p, axis=0, keepdims=True) for p in ps])
    out = functools.reduce(jnp.add, [_dot_tn(p.astype(BF16), v) for p, v in zip(ps, values)])
    inv_rows = jnp.concatenate(
        [jnp.broadcast_to(inv[:, c:c + LANES], (LANES, LANES)).T for c in range(0, 2 * m_q, LANES)], axis=0)
    out = out * inv_rows
    return jnp.where(_lane_iota((m_q, LANES)) < HALF, out[:m_q], out[m_q:])


def _na_kernel(q_ref, k_ref, v_ref, qc_ref, kc_ref, vc_ref, bias_ref, ox_ref, oc_ref, s_even, s_odd):
    n = q_ref.shape[0]
    rows = n // GRID_W
    win = NA_WIN_ROWS * GRID_W
    n_c = kc_ref.shape[0]

    def window_start(r):
        rs = jnp.clip(r - NA_WIN_ROWS // 2, 0, rows - NA_WIN_ROWS)
        return rs, pl.multiple_of(rs * GRID_W, GRID_W)

    def score_row(r, s_ref):
        rs, kstart = window_start(r)
        q_bd = _pair_block_diag(q_ref[pl.ds(pl.multiple_of(r * GRID_W, GRID_W), GRID_W), :])
        dr0 = rs - r + NA_WIN_ROWS - 1
        bias = jnp.concatenate([bias_ref[dr0 + a] for a in range(NA_WIN_ROWS)], axis=0)
        s_ref[0:win, :] = _dot_nt(k_ref[pl.ds(kstart, win), :], q_bd) + bias
        s_ref[win:win + n_c, :] = _dot_nt(kc_ref[...], q_bd)

    def finish_row(r, s_ref):
        _, kstart = window_start(r)
        out = _pair_softmax_pv([s_ref[0:win, :], s_ref[win:win + n_c, :]],
                               [v_ref[pl.ds(kstart, win), :], vc_ref[...]], GRID_W)
        ox_ref[pl.ds(pl.multiple_of(r * GRID_W, GRID_W), GRID_W), :] = out.astype(ox_ref.dtype)

    score_row(0, s_even)

    def row_pair(i, carry):
        r = 2 * i
        score_row(r + 1, s_odd)
        finish_row(r, s_even)
        score_row(jnp.minimum(r + 2, rows - 1), s_even)
        finish_row(r + 1, s_odd)
        return carry

    lax.fori_loop(0, rows // 2, row_pair, 0, unroll=4)

    q_bd = _pair_block_diag(qc_ref[...])
    oc_ref[...] = _pair_softmax_pv([_dot_nt(kc_ref[...], q_bd)], [vc_ref[...]], n_c).astype(oc_ref.dtype)


def _na(qkv_x, qkv_c, bias_tab, layer):
    bsz, n_x, _ = qkv_x.shape
    n_c = qkv_c.shape[1]
    n_pairs = NA_HEADS // NA_PAIR
    blk = lambda n, off: pl.BlockSpec((None, n, LANES), lambda b, p: (b, 0, off + p))
    return pl.pallas_call(
        _na_kernel,
        out_shape=[jax.ShapeDtypeStruct((bsz, n_x, NA_WIDTH), BF16), jax.ShapeDtypeStruct((bsz, n_c, NA_WIDTH), BF16)],
        grid=(bsz, n_pairs),
        in_specs=[blk(n_x, 0), blk(n_x, n_pairs), blk(n_x, 2 * n_pairs),
                  blk(n_c, 0), blk(n_c, n_pairs), blk(n_c, 2 * n_pairs),
                  pl.BlockSpec((None, None) + tuple(bias_tab.shape[2:]), lambda b, p: (layer, p, 0, 0, 0))],
        out_specs=[blk(n_x, 0), blk(n_c, 0)],
        scratch_shapes=[pltpu.VMEM((NA_WIN_ROWS * GRID_W + n_c, NA_PAIR * GRID_W), F32)] * 2,
        compiler_params=_params(32, 2),
        name="na_mixer",
    )(qkv_x, qkv_x, qkv_x, qkv_c, qkv_c, qkv_c, bias_tab)


def _na_bias_kernel(v_ref, o_ref):
    shape = (GRID_W, LANES)
    kc, lane = _row_iota(shape), _lane_iota(shape)
    j = lane % GRID_W
    off = kc - jnp.clip(j - NA_WIN_COLS // 2, 0, GRID_W - NA_WIN_COLS)
    for d in range(o_ref.shape[0]):
        t0 = pltpu.roll(jnp.broadcast_to(v_ref[0, d:d + 1, :], shape), 0, 1, stride=1, stride_axis=0)
        t1 = pltpu.roll(jnp.broadcast_to(v_ref[1, d:d + 1, :], shape), HALF, 1, stride=1, stride_axis=0)
        tile = jnp.where(lane < HALF, t0, t1)
        o_ref[d] = jnp.where(off >= 0, jnp.where(off < NA_WIN_COLS, tile, NEG_BIG), NEG_BIG)


def _na_bias_table(na_rpb):
    depth, heads, n_dr, n_dc = na_rpb.shape
    m = jnp.arange(LANES)
    t = jnp.where(m < HALF, m, m - LANES)
    rows = na_rpb[:, :, :, jnp.clip(NA_WIN_COLS - 1 - t, 0, n_dc - 1)]
    rows = jnp.pad(rows, ((0, 0), (0, 0), (0, NA_DR - n_dr), (0, 0)))
    return pl.pallas_call(
        _na_bias_kernel,
        out_shape=jax.ShapeDtypeStruct((depth, heads // NA_PAIR, n_dr, GRID_W, LANES), F32),
        grid=(depth, heads // NA_PAIR),
        in_specs=[pl.BlockSpec((None, NA_PAIR, NA_DR, LANES), lambda l, p: (l, p, 0, 0))],
        out_specs=pl.BlockSpec((None, None, n_dr, GRID_W, LANES), lambda l, p: (l, p, 0, 0, 0)),
        compiler_params=_params(16, 2),
        name="na_bias_table",
    )(rows)


POOL_TILES = POOL_WIDTH // LANES


def _pool_kernel(*refs):
    u_tiles, (w_ref, scale_ref, o_ref) = refs[:POOL_TILES], refs[POOL_TILES:]
    t = SSD_CHUNK
    n = o_ref.shape[0]
    n_chunks = n // t

    def tile_windows(p):
        lo_g, hi_g = (p * LANES) // POOL_GROUP, ((p + 1) * LANES - 1) // POOL_GROUP
        split = (hi_g * POOL_GROUP) - p * LANES if hi_g != lo_g else None
        return POOL_WINDOWS[lo_g], POOL_WINDOWS[hi_g], split

    def chunk(base, first, last):
        pos = base + _row_iota((t, 1))
        tiles = []
        for p, u_ref in enumerate(u_tiles):
            w_lo, w_hi, split = tile_windows(p)
            x0 = u_ref[pl.ds(base, t), :]
            sums, acc, have = {}, x0, 1
            for w in POOL_WINDOWS:
                if w > w_hi:
                    break
                for d in list(range(-(w // 2), -(have // 2))) + list(range(have - have // 2, w // 2)):
                    acc = acc + _shifted_rows(u_ref, base, d, first, last)
                have = w
                cnt = (jnp.minimum(pos + w // 2, n) - jnp.maximum(pos - w // 2, 0)).astype(F32)
                sums[w] = acc / cnt
            if split is None:
                pooled = sums[w_lo]
            else:
                pooled = jnp.where(_lane_iota((t, LANES)) < split, sums[w_lo], sums[w_hi])
            tiles.append((pooled - x0).astype(BF16))
        mapped = _dot(jnp.concatenate(tiles, axis=1), w_ref[...])
        o_ref[pl.ds(base, t), :] = (mapped * scale_ref[...]).astype(o_ref.dtype)

    chunk(0, True, False)

    def body(c, carry):
        chunk(pl.multiple_of(c * t, t), False, False)
        return carry

    lax.fori_loop(1, n_chunks - 1, body, 0)
    chunk((n_chunks - 1) * t, False, True)


def _pool(u, w_bd, scale, layer):
    bsz, n, _ = u.shape
    return pl.pallas_call(
        _pool_kernel,
        out_shape=jax.ShapeDtypeStruct((bsz, n, POOL_WIDTH), BF16),
        grid=(bsz,),
        in_specs=[pl.BlockSpec((None, n, LANES), lambda b, p=p: (b, 0, p)) for p in range(POOL_TILES)]
        + [_layer_resident((POOL_WIDTH, POOL_WIDTH), layer), _layer_resident((1, POOL_WIDTH), layer)],
        out_specs=pl.BlockSpec((None, n, POOL_WIDTH), lambda b: (b, 0, 0)),
        compiler_params=_params(40),
        name="pool_mixer",
    )(*([u] * POOL_TILES), w_bd, scale)


FNET_TILE_X = 256
FNET_TILE_C = 128
FNET_TILE_PAD = 16
FNET_CHAN_ROWS = 256


def _fnet_kernel(u_ref, cc_ref, sc_ref, pos_ref, flip_ref, o_ref, z_s, sum_s):
    n = u_ref.shape[0]
    tq = o_ref.shape[0]
    n_lo = sum_s.shape[0]
    t = pl.program_id(1)

    @pl.when(t == 0)
    def _():
        for r in range(0, n, min(n, FNET_CHAN_ROWS)):
            ub = u_ref[r:r + min(n, FNET_CHAN_ROWS), :].astype(BF16)
            z_s[r:r + ub.shape[0], :] = _dot(ub, cc_ref[...]).astype(BF16)
            z_s[n + r:n + r + ub.shape[0], :] = _dot(ub, sc_ref[...]).astype(BF16)

    scale = 1.0 / math.sqrt(n * FNET_HEAD_DIM)

    @pl.when(t < n_lo)
    def _():
        p = _dot(pos_ref[:, 0:n], z_s[0:n, :])
        q = _dot(pos_ref[:, n:2 * n], z_s[n:2 * n, :])
        o_ref[...] = ((p - q)[0:tq] * scale).astype(o_ref.dtype)
        sum_s[t] = ((p + q) * scale).astype(sum_s.dtype)

    @pl.when(t >= n_lo)
    def _():
        o_ref[...] = _dot(flip_ref[...], sum_s[2 * n_lo - 1 - t]).astype(o_ref.dtype)


def _fnet(u, chan_cos, chan_sin, pos_tab, flip):
    bsz, n, _ = u.shape
    n_lo, tq_pad, _ = pos_tab.shape
    tq = tq_pad - FNET_TILE_PAD
    return pl.pallas_call(
        _fnet_kernel,
        out_shape=jax.ShapeDtypeStruct((bsz, n, FNET_WIDTH), BF16),
        grid=(bsz, 2 * n_lo),
        in_specs=[pl.BlockSpec((None, n, FNET_WIDTH), lambda b, j: (b, 0, 0)),
                  _resident((FNET_WIDTH, FNET_WIDTH)),
                  _resident((FNET_WIDTH, FNET_WIDTH)),
                  pl.BlockSpec((None, tq_pad, 2 * n), lambda b, j: (jnp.minimum(j, n_lo - 1), 0, 0)),
                  _resident((tq, tq_pad))],
        out_specs=pl.BlockSpec((None, tq, FNET_WIDTH), lambda b, j: (b, j, 0)),
        scratch_shapes=[pltpu.VMEM((2 * n, FNET_WIDTH), BF16), pltpu.VMEM((n_lo, tq_pad, FNET_WIDTH), BF16)],
        compiler_params=_params(48, 2),
        name="fnet_mixer",
    )(u, chan_cos, chan_sin, pos_tab, flip)


def _dft_cos_sin(rows, n):
    m = jnp.arange(n, dtype=jnp.int32)
    ang = ((rows[:, None] * m[None, :]) % n).astype(F32) * (2.0 * math.pi / n)
    return jnp.cos(ang), jnp.sin(ang)


def _fnet_tables(n, tq):
    n_lo = n // (2 * tq)
    rows = (jnp.arange(n_lo, dtype=jnp.int32)[:, None] * tq + jnp.arange(tq + FNET_TILE_PAD, dtype=jnp.int32)[None, :])
    c, s = _dft_cos_sin(rows.reshape(-1), n)
    tab = jnp.concatenate([c, s], axis=1).astype(BF16).reshape(n_lo, tq + FNET_TILE_PAD, 2 * n)
    i = jnp.arange(tq, dtype=jnp.int32)[:, None]
    src = jnp.arange(tq + FNET_TILE_PAD, dtype=jnp.int32)[None, :]
    return tab, (src == tq - i).astype(BF16)


def _fnet_channel_tables():
    c, s = _dft_cos_sin(jnp.arange(FNET_HEAD_DIM, dtype=jnp.int32), FNET_HEAD_DIM)
    eye = jnp.eye(FNET_HEADS, dtype=F32)
    return jnp.kron(eye, c).astype(BF16), jnp.kron(eye, s).astype(BF16)


def _merge_kernel(x_ref, mod_ref, ssd_ref, na_ref, pool_ref, fnet_ref,
                  wg_ref, bg_ref, wssd_ref, wna_ref, wpool_ref, wfnet_ref, wmix_ref, g_ref, b_ref, o_ref):
    x = x_ref[...]
    h = (_ln(x) * (1.0 + mod_ref[1:2, :]) + mod_ref[0:1, :]).astype(BF16)
    merged = None
    branches = ((ssd_ref, wssd_ref), (na_ref, wna_ref), (pool_ref, wpool_ref), (fnet_ref, wfnet_ref))
    for i, (y_ref, w_ref) in enumerate(branches):
        cols = slice(i * D_MODEL, (i + 1) * D_MODEL)
        gate = jax.nn.sigmoid(_dot(h, wg_ref[:, cols]) + bg_ref[:, cols])
        term = gate * _dot(y_ref[...], w_ref[...])
        merged = term if merged is None else merged + term
    mix = _dot(merged.astype(BF16), wmix_ref[...])
    y = _ln(DEEPNORM_ALPHA * x + mod_ref[2:3, :] * mix)
    o_ref[...] = y * g_ref[...] + b_ref[...]


def _merge(x2d, mod, tiles_per_mod, layer, ssd, na, pool, fnet, weights, tm):
    m = x2d.shape[0]
    mod_map = (lambda i: (0, 0, 0)) if tiles_per_mod is None else (lambda i: (i // tiles_per_mod, 0, 0))
    tok = lambda n: pl.BlockSpec((tm, n), lambda i: (i, 0))
    wg, bg, wssd, wna, wpool, wfnet, wmix, ln_g, ln_b = weights
    in_specs = [tok(D_MODEL), pl.BlockSpec((None, 8, D_MODEL), mod_map),
                tok(SSD_D_INNER), tok(NA_WIDTH), tok(POOL_WIDTH), tok(FNET_WIDTH),
                _layer_resident((D_MODEL, N_BRANCH * D_MODEL), layer),
                _layer_resident((1, N_BRANCH * D_MODEL), layer),
                _layer_resident((SSD_D_INNER, D_MODEL), layer),
                _layer_resident((NA_WIDTH, D_MODEL), layer),
                _layer_resident((POOL_WIDTH, D_MODEL), layer),
                _layer_resident((FNET_WIDTH, D_MODEL), layer),
                _layer_resident((D_MODEL, D_MODEL), layer),
                _layer_resident((1, D_MODEL), layer),
                _layer_resident((1, D_MODEL), layer)]
    return pl.pallas_call(
        _merge_kernel,
        out_shape=jax.ShapeDtypeStruct((m, D_MODEL), F32),
        grid=(m // tm,),
        in_specs=in_specs,
        out_specs=tok(D_MODEL),
        compiler_params=_params(58),
        name="merge_mix",
    )(x2d, mod, ssd, na, pool, fnet, wg, bg, wssd, wna, wpool, wfnet, wmix, ln_g, ln_b)


MLP_CHUNK = 1024


def _mlp_kernel(x_ref, mod_ref, wup_ref, bup_ref, wdn_ref, bdn_ref, g_ref, b_ref, o_ref):
    x = x_ref[...]
    h = (_ln(x) * (1.0 + mod_ref[4:5, :]) + mod_ref[3:4, :]).astype(BF16)
    acc = None
    for c in range(D_FF // MLP_CHUNK):
        cols = slice(c * MLP_CHUNK, (c + 1) * MLP_CHUNK)
        a = jnp.maximum(_dot(h, wup_ref[:, cols]) + bup_ref[:, cols], 0.0)
        part = _dot((a * a).astype(BF16), wdn_ref[cols, :])
        acc = part if acc is None else acc + part
    y = _ln(DEEPNORM_ALPHA * x + mod_ref[5:6, :] * (acc + bdn_ref[...]))
    o_ref[...] = y * g_ref[...] + b_ref[...]


def _mlp(x2d, mod, tiles_per_mod, layer, weights, tm):
    m = x2d.shape[0]
    mod_map = (lambda i: (0, 0, 0)) if tiles_per_mod is None else (lambda i: (i // tiles_per_mod, 0, 0))
    tok = pl.BlockSpec((tm, D_MODEL), lambda i: (i, 0))
    wup, bup, wdn, bdn, ln_g, ln_b = weights
    in_specs = [tok, pl.BlockSpec((None, 8, D_MODEL), mod_map),
                _layer_resident((D_MODEL, D_FF), layer), _layer_resident((1, D_FF), layer),
                _layer_resident((D_FF, D_MODEL), layer), _layer_resident((1, D_MODEL), layer),
                _layer_resident((1, D_MODEL), layer), _layer_resident((1, D_MODEL), layer)]
    return pl.pallas_call(
        _mlp_kernel,
        out_shape=jax.ShapeDtypeStruct((m, D_MODEL), F32),
        grid=(m // tm,),
        in_specs=in_specs,
        out_specs=tok,
        compiler_params=_params(58),
        name="mlp",
    )(x2d, mod, wup, bup, wdn, bdn, ln_g, ln_b)


def _rope_tables(n):
    pos = jnp.arange(n)
    quarter = SSD_STATE // 4
    inv_freq = ROPE_THETA ** (-jnp.arange(quarter, dtype=F32) / quarter)
    row = (pos // GRID_W).astype(F32)[:, None] * inv_freq
    col = (pos % GRID_W).astype(F32)[:, None] * inv_freq
    ang = jnp.concatenate([row, col], -1)
    cos, sin = jnp.cos(ang), jnp.sin(ang)
    return jnp.concatenate([cos, cos], -1), jnp.concatenate([-sin, sin], -1)


def _dt_lane_layout(a):
    lead = a.shape[:-2]
    a = a.reshape(lead + (2, SSD_GROUPS, HEADS_PER_GROUP))
    a = jnp.swapaxes(a, -3, -2).reshape(lead + (2 * SSD_HEADS,))
    return jnp.pad(a, [(0, 0)] * len(lead) + [(0, DT_PAD - 2 * SSD_HEADS)])


def _block_diag(w):
    l, g, c, _ = w.shape
    eye = jnp.eye(g, dtype=w.dtype)
    return jnp.einsum("lgcd,gh->lgchd", w, eye).reshape(l, g * c, g * c)


def kernel(x, c, ctx, c_ctx, ada_w, ada_b, in_w, in_b, ssd_conv_w, ssd_conv_b, ssd_dt_bias, ssd_a_log, ssd_d,
           ssd_norm_w, ssd_out_w, na_rpb, na_out_w, pool_w, pool_scale, pool_out_w, fnet_out_w, mix_out_w,
           ln1_g, ln1_b, mlp_up_w, mlp_up_b, mlp_down_w, mlp_down_b, ln2_g, ln2_b):
    bsz, n_x, d = x.shape
    n_c = ctx.shape[1]
    depth = in_w.shape[0]

    offs = [0]
    for wdt in (SSD_D_INNER, SSD_XBC, 2 * SSD_HEADS, 3 * NA_WIDTH, POOL_WIDTH, FNET_WIDTH, N_BRANCH * D_MODEL):
        offs.append(offs[-1] + wdt)

    def in_cols(a, i):
        return a[..., offs[i]:offs[i + 1]]

    def split_in(a):
        dt_cols = in_cols(a, 2)
        dt_cols = _dt_lane_layout(dt_cols.reshape(dt_cols.shape[:-1] + (2, SSD_HEADS)))
        qkv = in_cols(a, 3)
        q_scale = jnp.concatenate([jnp.full((NA_WIDTH,), NA_HEAD_DIM ** -0.5, F32), jnp.ones((2 * NA_WIDTH,), F32)])
        return [in_cols(a, 0), in_cols(a, 1), dt_cols, qkv * q_scale, in_cols(a, 4), in_cols(a, 5)], in_cols(a, 6)

    w_parts, w_gate = split_in(in_w)
    b_parts, b_gate = split_in(in_b[:, None, :])
    w_parts = [w.astype(BF16) for w in w_parts]
    w_gate = w_gate.astype(BF16)

    conv_w = jnp.pad(ssd_conv_w, ((0, 0), (0, 8 - SSD_CONV), (0, 0)))
    conv_b = ssd_conv_b[:, None, :]
    dt_bias = _dt_lane_layout(ssd_dt_bias)[:, None, :]
    a_log = _dt_lane_layout(ssd_a_log)[:, None, :]
    d_ch = jnp.repeat(ssd_d, SSD_HEADDIM, axis=-1)[:, None, :]
    norm_w = ssd_norm_w[:, None, :]
    cos2, sin2 = _rope_tables(n_x)

    bias_tab = _na_bias_table(na_rpb)
    pool_bd = _block_diag(pool_w).astype(BF16)
    pool_sc = pool_scale[:, None, :]
    chan_cos, chan_sin = _fnet_channel_tables()
    pos_tab_x, flip_x = _fnet_tables(n_x, FNET_TILE_X)
    pos_tab_c, flip_c = _fnet_tables(n_c, FNET_TILE_C)

    merge_w = [w_gate, b_gate, ssd_out_w.astype(BF16), na_out_w.astype(BF16), pool_out_w.astype(BF16),
               fnet_out_w.astype(BF16), mix_out_w.astype(BF16), ln1_g[:, None, :], ln1_b[:, None, :]]
    mlp_w = [mlp_up_w.astype(BF16), mlp_up_b[:, None, :], mlp_down_w.astype(BF16), mlp_down_b[:, None, :],
             ln2_g[:, None, :], ln2_b[:, None, :]]

    rows = ((bsz + 1 + 7) // 8) * 8
    c_all = jnp.zeros((rows, d), F32).at[:bsz].set(c).at[bsz].set(c_ctx)
    mods = _ada_all_layers(c_all, ada_w, ada_b).reshape(depth, rows, 6, d)
    mods = jnp.pad(mods, ((0, 0), (0, 0), (0, 2), (0, 0)))

    tm = TOKEN_TILE
    tm_c = min(tm, bsz * n_c)
    assert n_x % tm == 0 and (bsz * n_c) % tm_c == 0
    tiles_x = n_x // tm
    xf = x.reshape(bsz * n_x, d)
    cf = ctx.reshape(bsz * n_c, d)
    for i in range(depth):
        need_ctx = i < depth - 1
        mod_x, mod_c = mods[i, :bsz], mods[i, bsz:bsz + 1]
        z_x, xbc_x, dt_x, qkv_x, pool_x, fnet_x = _inproj(xf, mod_x, tiles_x, i, w_parts, b_parts, tm)
        z_c, xbc_c, dt_c, qkv_c, pool_c, fnet_c = _inproj(cf, mod_c, None, i, w_parts, b_parts, tm_c)
        r3 = lambda a, n: a.reshape(bsz, n, a.shape[-1])
        ssd_x, ssd_c = _ssd(r3(xbc_x, n_x), r3(dt_x, n_x), r3(z_x, n_x), r3(xbc_c, n_c), r3(dt_c, n_c), r3(z_c, n_c),
                            conv_w, conv_b, dt_bias, a_log, d_ch, norm_w, cos2, sin2, i)
        na_x, na_c = _na(r3(qkv_x, n_x), r3(qkv_c, n_c), bias_tab, i)
        pm_x = _pool(r3(pool_x, n_x), pool_bd, pool_sc, i)
        fm_x = _fnet(r3(fnet_x, n_x), chan_cos, chan_sin, pos_tab_x, flip_x)
        f2 = lambda a: a.reshape(-1, a.shape[-1])
        x1 = _merge(xf, mod_x, tiles_x, i, f2(ssd_x), f2(na_x), f2(pm_x), f2(fm_x), merge_w, tm)
        xf = _mlp(x1, mod_x, tiles_x, i, mlp_w, tm)
        if need_ctx:
            pm_c = _pool(r3(pool_c, n_c), pool_bd, pool_sc, i)
            fm_c = _fnet(r3(fnet_c, n_c), chan_cos, chan_sin, pos_tab_c, flip_c)
            c1 = _merge(cf, mod_c, None, i, f2(ssd_c), f2(na_c), f2(pm_c), f2(fm_c), merge_w, tm_c)
            cf = _mlp(c1, mod_c, None, i, mlp_w, tm_c)
    return xf.reshape(bsz, n_x, d)
```

```python
import functools
import math

import jax
import jax.numpy as jnp
from jax import lax
from jax.experimental import pallas as pl
from jax.experimental.pallas import tpu as pltpu

F32 = jnp.float32
BF16 = jnp.bfloat16

D_MODEL = 1024
DEPTH = 4
GRID_W = 64

SSD_HEADS = 16
SSD_HEADDIM = 64
SSD_D_INNER = SSD_HEADS * SSD_HEADDIM
SSD_GROUPS = 4
SSD_STATE = 128
SSD_CONV = 5
SSD_CHUNK = 128
SSD_XBC = SSD_D_INNER + 2 * SSD_GROUPS * SSD_STATE
HEADS_PER_GROUP = SSD_HEADS // SSD_GROUPS
GROUP_WIDTH = HEADS_PER_GROUP * SSD_HEADDIM

NA_HEADS = 16
NA_HEAD_DIM = 64
NA_WIDTH = NA_HEADS * NA_HEAD_DIM
NA_WIN_ROWS = 8
NA_WIN_COLS = 16

POOL_WINDOWS = (2, 4, 8, 16)
POOL_GROUP = 192
POOL_WIDTH = 4 * POOL_GROUP

FNET_HEADS = 4
FNET_HEAD_DIM = 192
FNET_WIDTH = FNET_HEADS * FNET_HEAD_DIM

N_BRANCH = 4
D_FF = 4 * D_MODEL
ROPE_THETA = 10000.0
DEEPNORM_ALPHA = (2 * DEPTH) ** 0.25
LN_EPS = 1e-6
RMS_EPS = 1e-5

LANES = 128
HALF = LANES // 2
DT_PAD = LANES
TOKEN_TILE = 512
NEG_BIG = -1e30
LOG2_E = 1.4426950408889634
MIB = 1024 * 1024


def _resident(shape):
    nd = len(shape)
    return pl.BlockSpec(shape, lambda *_: (0,) * nd, pipeline_mode=pl.Buffered(1))


def _layer_resident(shape, layer):
    nd = len(shape)
    return pl.BlockSpec((None,) + tuple(shape), lambda *_: (layer,) + (0,) * nd, pipeline_mode=pl.Buffered(1))


def _params(vmem_mib, n_axes=1):
    return pltpu.CompilerParams(dimension_semantics=("arbitrary",) * n_axes, vmem_limit_bytes=vmem_mib * MIB)


def _ln(x):
    mu = jnp.mean(x, axis=-1, keepdims=True)
    xc = x - mu
    var = jnp.mean(xc * xc, axis=-1, keepdims=True)
    return xc * lax.rsqrt(var + LN_EPS)


def _silu(x):
    return x * jax.nn.sigmoid(x)


def _dot(a, b):
    return jnp.dot(a, b, preferred_element_type=F32)


def _dot_nt(a, b):
    return lax.dot_general(a, b, (((1,), (1,)), ((), ())), preferred_element_type=F32)


def _ada_kernel(c_ref, w_ref, b_ref, o_ref):
    a = _silu(c_ref[...]).astype(BF16)
    o_ref[...] = _dot(a, w_ref[...].astype(BF16)) + b_ref[...]


def _ada_all_layers(c_all, ada_w, ada_b):
    rows = c_all.shape[0]
    return pl.pallas_call(
        _ada_kernel,
        out_shape=jax.ShapeDtypeStruct((DEPTH, rows, 6 * D_MODEL), F32),
        grid=(DEPTH, 6),
        in_specs=[
            _resident((rows, D_MODEL)),
            pl.BlockSpec((None, D_MODEL, D_MODEL), lambda l, j: (l, 0, j)),
            pl.BlockSpec((None, 1, D_MODEL), lambda l, j: (l, 0, j)),
        ],
        out_specs=pl.BlockSpec((None, rows, D_MODEL), lambda l, j: (l, 0, j)),
        compiler_params=_params(32, 2),
        name="ada_mod",
    )(c_all, ada_w, ada_b.reshape(DEPTH, 1, 6 * D_MODEL))


IN_SPLITS = (SSD_D_INNER, SSD_XBC, DT_PAD, 3 * NA_WIDTH, POOL_WIDTH, FNET_WIDTH)
IN_DTYPES = (F32, F32, F32, BF16, F32, F32)


def _inproj_kernel(x_ref, mod_ref, *refs):
    n = len(IN_SPLITS)
    w_refs, b_refs, o_refs = refs[:n], refs[n:2 * n], refs[2 * n:]
    h = (_ln(x_ref[...]) * (1.0 + mod_ref[1:2, :]) + mod_ref[0:1, :]).astype(BF16)
    for w_ref, b_ref, o_ref in zip(w_refs, b_refs, o_refs):
        o_ref[...] = (_dot(h, w_ref[...]) + b_ref[...]).astype(o_ref.dtype)


def _inproj(x2d, mod, tiles_per_mod, layer, weights, biases, tm):
    m = x2d.shape[0]
    if tiles_per_mod is None:
        mod_map = lambda i: (0, 0, 0)
    else:
        mod_map = lambda i: (i // tiles_per_mod, 0, 0)
    in_specs = [pl.BlockSpec((tm, D_MODEL), lambda i: (i, 0)), pl.BlockSpec((None, 8, D_MODEL), mod_map)]
    in_specs += [_layer_resident((D_MODEL, n), layer) for n in IN_SPLITS]
    in_specs += [_layer_resident((1, n), layer) for n in IN_SPLITS]
    return pl.pallas_call(
        _inproj_kernel,
        out_shape=[jax.ShapeDtypeStruct((m, n), dt) for n, dt in zip(IN_SPLITS, IN_DTYPES)],
        grid=(m // tm,),
        in_specs=in_specs,
        out_specs=[pl.BlockSpec((tm, n), lambda i: (i, 0)) for n in IN_SPLITS],
        compiler_params=_params(58),
        name="in_proj",
    )(x2d, mod, *weights, *biases)


def _row_iota(shape):
    return lax.broadcasted_iota(jnp.int32, shape, 0)


def _lane_iota(shape):
    return lax.broadcasted_iota(jnp.int32, shape, 1)


def _shifted_rows(ref, base, d, first, last):
    t = SSD_CHUNK
    if d == 0 or not ((first and d < 0) or (last and d > 0)):
        return ref[pl.ds(base + d, t), :]
    x = ref[pl.ds(base, t), :]
    row = _row_iota(x.shape)
    if d < 0:
        return jnp.where(row >= -d, pltpu.roll(x, -d, 0), 0.0)
    return jnp.where(row < t - d, pltpu.roll(x, t - d, 0), 0.0)


def _split3(x):
    hi = x.astype(BF16)
    r1 = x - hi.astype(F32)
    mid = r1.astype(BF16)
    lo = (r1 - mid.astype(F32)).astype(BF16)
    return hi, mid, lo


def _ssd_kernel(xs0_x, xs1_x, bm_x, cm_x, dt_x, z_x, xs0_c, xs1_c, bm_c, cm_c, dt_c, z_c,
                wxs, wbm, wcm, bxs, bbm, bcm, dtb_ref, alog_ref, dch_ref, nw_ref, cos_ref, sin_ref,
                ox_ref, oc_ref,
                xs_s, b_s, bt_s, c_s, dtt_s, ac_s, act_s, y_s, ea_s, sc_s, cd_s, st_s):
    t = SSD_CHUNK
    n_x, n_c = bm_x.shape[0], bm_c.shape[0]
    nc_x, nc_c = n_x // t, n_c // t
    g = pl.program_id(1)
    half = _lane_iota((t, LANES)) < HALF
    half_row = _lane_iota((1, LANES)) < HALF
    li, si = _row_iota((t, t)), _lane_iota((t, t))
    tril = jnp.where(si <= li, 1.0, 0.0).astype(BF16)
    triu = jnp.where(si >= li, 1.0, 0.0).astype(BF16)
    is_fwd_lane = (_lane_iota((t, LANES)) % (2 * HEADS_PER_GROUP)) < HEADS_PER_GROUP
    a_row = -jnp.exp(alog_ref[...])

    def conv_silu(ref, w_ref, b_ref, base, lanes, first, last):
        acc = b_ref[:, lanes] + w_ref[0:1, lanes] * _shifted_rows(ref, base, -(SSD_CONV // 2), first, last)
        for k in range(1, SSD_CONV):
            acc = acc + w_ref[k:k + 1, lanes] * _shifted_rows(ref, base, k - SSD_CONV // 2, first, last)
        return _silu(acc)

    def prep_chunk(refs, base, u0, first, last, rope):
        xs_tiles, bm_r, cm_r = refs
        dst = pl.ds(u0 + base, t)
        for p, xs_r in enumerate(xs_tiles):
            lanes = slice(p * LANES, (p + 1) * LANES)
            xs_s[dst, lanes] = conv_silu(xs_r, wxs, bxs, base, lanes, first, last)
        bm = conv_silu(bm_r, wbm, bbm, base, slice(None), first, last)
        cm = conv_silu(cm_r, wcm, bcm, base, slice(None), first, last)
        if rope:
            cos2, sin2 = cos_ref[pl.ds(base, t), :], sin_ref[pl.ds(base, t), :]
            bm = bm * cos2 + pltpu.roll(bm, HALF, 1) * sin2
            cm = cm * cos2 + pltpu.roll(cm, HALF, 1) * sin2
        b_s[dst, :] = bm.astype(BF16)
        bt_s[dst, :] = bm.T
        c_s[dst, :] = cm.astype(BF16)

    def prep_dt(dt_r, n_chunks, u0):
        def body(c, carry):
            base = pl.multiple_of(c * t, t)
            dst = pl.ds(u0 + base, t)
            raw = dt_r[pl.ds(base, t), :] + dtb_ref[...]
            dt = jnp.maximum(raw, 0.0) + jnp.log1p(jnp.exp(-jnp.abs(raw)))
            dtt_s[dst, :] = dt.T
            parts = _split3(dt * a_row)
            prefix = _dot(tril, parts[0]) + _dot(tril, parts[1]) + _dot(tril, parts[2])
            suffix = _dot(triu, parts[0]) + _dot(triu, parts[1]) + _dot(triu, parts[2])
            acum = jnp.where(is_fwd_lane, prefix, suffix) * LOG2_E
            ac_s[dst, :] = acum
            act_s[dst, :] = acum.T
            return carry

        lax.fori_loop(0, n_chunks, body, 0, unroll=2)

    @pl.when(g == 0)
    def _():
        prep_dt(dt_c, nc_c, 0)
        prep_dt(dt_x, nc_x, n_c)

    def prep_seq(refs, n_chunks, u0, rope):
        prep_chunk(refs, 0, u0, True, False, rope)

        def body(c, carry):
            prep_chunk(refs, pl.multiple_of(c * t, t), u0, False, False, rope)
            return carry

        lax.fori_loop(1, n_chunks - 1, body, 0, unroll=2)
        prep_chunk(refs, (n_chunks - 1) * t, u0, False, True, rope)

    prep_seq(((xs0_c, xs1_c), bm_c, cm_c), nc_c, 0, False)
    prep_seq(((xs0_x, xs1_x), bm_x, cm_x), nc_x, n_c, True)

    n_chunks = nc_c + nc_x

    def local_chunk(u, carry):
        r0 = pl.multiple_of(u * t, t)
        rows = pl.ds(r0, t)
        cb = _dot_nt(c_s[rows, :], b_s[rows, :])
        acum, bt = ac_s[rows, :], bt_s[rows, :]
        lane = _lane_iota((t, LANES))
        x_stacked = []
        for p in range(GROUP_WIDTH // LANES):
            xb = xs_s[rows, p * LANES:(p + 1) * LANES].astype(BF16)
            zero = jnp.zeros_like(xb)
            x_stacked.append(jnp.concatenate([jnp.where(half, xb, zero), jnp.where(half, zero, xb)], axis=0))
        y_tiles = [None] * (GROUP_WIDTH // LANES)
        for dirn in range(2):
            tri = (si <= li) if dirn == 0 else (si >= li)
            end_lane = t - 1 if dirn == 0 else 0
            for p in range(GROUP_WIDTH // LANES):
                lanes = slice(p * LANES, (p + 1) * LANES)
                acols, aends, m_heads, bt_heads = [], [], [], []
                for jj in range(2):
                    ln = g * (2 * HEADS_PER_GROUP) + dirn * HEADS_PER_GROUP + 2 * p + jj
                    acol = jnp.sum(jnp.where(lane == ln, acum, 0.0), axis=1, keepdims=True)
                    arow = act_s[pl.ds(r0 + ln, 1), :]
                    dtrow = dtt_s[pl.ds(r0 + ln, 1), :]
                    aend = jnp.sum(jnp.where(_lane_iota((1, t)) == end_lane, arow, 0.0), axis=1, keepdims=True)
                    decay_dt = jnp.exp2(jnp.where(tri, acol - (arow - jnp.log2(dtrow)), NEG_BIG))
                    m_heads.append((cb * decay_dt).astype(BF16))
                    bt_heads.append((bt * (dtrow * jnp.exp2(aend - arow))).astype(BF16))
                    acols.append(acol)
                    aends.append(aend)
                y = _dot(jnp.concatenate(m_heads, axis=1), x_stacked[p])
                y_tiles[p] = y if y_tiles[p] is None else y_tiles[p] + y
                sc_s[dirn, rows, lanes] = _dot(jnp.concatenate(bt_heads, axis=1), x_stacked[p])
                ea_s[dirn, rows, lanes] = jnp.exp2(jnp.where(half, acols[0], acols[1]))
                cd = jnp.exp2(jnp.where(half_row, aends[0], aends[1]))
                cd_s[dirn, pl.ds(pl.multiple_of(u * 8, 8), 8), lanes] = jnp.broadcast_to(cd, (8, LANES))
        for p, y in enumerate(y_tiles):
            y_s[rows, p * LANES:(p + 1) * LANES] = y
        return carry

    lax.fori_loop(0, n_chunks, local_chunk, 0, unroll=2)

    def carry_step(i, carry):
        u_fwd = i
        u_bwd = jnp.where(i < nc_c, nc_c - 1 - i, n_chunks + nc_c - 1 - i)
        for dirn, u in ((0, u_fwd), (1, u_bwd)):
            rows = pl.ds(pl.multiple_of(u * t, t), t)
            st = st_s[dirn]
            cd = cd_s[dirn, pl.ds(pl.multiple_of(u * 8, 8), 1), :]
            st_s[dirn] = st * cd + sc_s[dirn, rows, :]
            sc_s[dirn, rows, :] = st
        return carry

    st_s[...] = jnp.zeros_like(st_s)
    lax.fori_loop(0, n_chunks, carry_step, 0)

    def finish(z_ref, o_ref, n_chunks, u0):
        def body(c, carry):
            base = pl.multiple_of(c * t, t)
            rows = pl.ds(u0 + base, t)
            zz = z_ref[pl.ds(base, t), :]
            y = y_s[rows, :] + dch_ref[...] * xs_s[rows, :]
            for dirn in range(2):
                y = y + _dot(c_s[rows, :], sc_s[dirn, rows, :].astype(BF16)) * ea_s[dirn, rows, :]
            y = y * _silu(zz)
            ms = jnp.mean(y * y, axis=-1, keepdims=True)
            o_ref[pl.ds(base, t), :] = (y * lax.rsqrt(ms + RMS_EPS) * nw_ref[...]).astype(o_ref.dtype)
            return carry

        lax.fori_loop(0, n_chunks, body, 0, unroll=2)

    finish(z_c, oc_ref, nc_c, 0)
    finish(z_x, ox_ref, nc_x, n_c)


def _ssd(xbc_x, dt_x, z_x, xbc_c, dt_c, z_c, conv_w, conv_b, dt_bias, a_log, d_ch, norm_w, cos2, sin2, layer):
    bsz, n_x, _ = xbc_x.shape
    n_c = xbc_c.shape[1]
    n_all = n_x + n_c
    gw, st = GROUP_WIDTH, SSD_STATE
    b_blk0 = SSD_D_INNER // st
    c_blk0 = b_blk0 + SSD_GROUPS

    def seq_specs(n):
        return [
            pl.BlockSpec((None, n, LANES), lambda b, g: (b, 0, 2 * g)),
            pl.BlockSpec((None, n, LANES), lambda b, g: (b, 0, 2 * g + 1)),
            pl.BlockSpec((None, n, st), lambda b, g: (b, 0, b_blk0 + g)),
            pl.BlockSpec((None, n, st), lambda b, g: (b, 0, c_blk0 + g)),
            pl.BlockSpec((None, n, DT_PAD), lambda b, g: (b, 0, 0)),
            pl.BlockSpec((None, n, gw), lambda b, g: (b, 0, g)),
        ]

    in_specs = seq_specs(n_x) + seq_specs(n_c) + [
        pl.BlockSpec((None, 8, gw), lambda b, g: (layer, 0, g)),
        pl.BlockSpec((None, 8, st), lambda b, g: (layer, 0, b_blk0 + g)),
        pl.BlockSpec((None, 8, st), lambda b, g: (layer, 0, c_blk0 + g)),
        pl.BlockSpec((None, 1, gw), lambda b, g: (layer, 0, g)),
        pl.BlockSpec((None, 1, st), lambda b, g: (layer, 0, b_blk0 + g)),
        pl.BlockSpec((None, 1, st), lambda b, g: (layer, 0, c_blk0 + g)),
        pl.BlockSpec((None, 1, DT_PAD), lambda b, g: (layer, 0, 0)),
        pl.BlockSpec((None, 1, DT_PAD), lambda b, g: (layer, 0, 0)),
        pl.BlockSpec((None, 1, gw), lambda b, g: (layer, 0, g)),
        pl.BlockSpec((None, 1, gw), lambda b, g: (layer, 0, g)),
        _resident((n_x, st)),
        _resident((n_x, st)),
    ]
    out_shape = [jax.ShapeDtypeStruct((bsz, n_x, SSD_D_INNER), BF16),
                 jax.ShapeDtypeStruct((bsz, n_c, SSD_D_INNER), BF16)]
    out_specs = [pl.BlockSpec((None, n_x, gw), lambda b, g: (b, 0, g)),
                 pl.BlockSpec((None, n_c, gw), lambda b, g: (b, 0, g))]
    scratch = [
        pltpu.VMEM((n_all, gw), F32),
        pltpu.VMEM((n_all, st), BF16),
        pltpu.VMEM((n_all, st), F32),
        pltpu.VMEM((n_all, st), BF16),
        pltpu.VMEM((n_all, DT_PAD), F32),
        pltpu.VMEM((n_all, DT_PAD), F32),
        pltpu.VMEM((n_all, DT_PAD), F32),
        pltpu.VMEM((n_all, gw), F32),
        pltpu.VMEM((2, n_all, gw), F32),
        pltpu.VMEM((2, n_all, gw), F32),
        pltpu.VMEM((2, 8 * (n_all // SSD_CHUNK), gw), F32),
        pltpu.VMEM((2, st, gw), F32),
    ]
    return pl.pallas_call(
        _ssd_kernel,
        out_shape=out_shape,
        grid=(bsz, SSD_GROUPS),
        in_specs=in_specs,
        out_specs=out_specs,
        scratch_shapes=scratch,
        compiler_params=_params(52, 2),
        name="ssd_mixer",
    )(xbc_x, xbc_x, xbc_x, xbc_x, dt_x, z_x, xbc_c, xbc_c, xbc_c, xbc_c, dt_c, z_c,
      conv_w, conv_w, conv_w, conv_b, conv_b, conv_b, dt_bias, a_log, d_ch, norm_w, cos2, sin2)


NA_PAIR = 2
NA_DR = 2 * NA_WIN_ROWS


def _dot_tn(a, b):
    return lax.dot_general(a, b, (((0,), (0,)), ((), ())), preferred_element_type=F32)


def _pair_block_diag(q):
    lane = _lane_iota(q.shape)
    zero = jnp.zeros_like(q)
    return jnp.concatenate([jnp.where(lane < HALF, q, zero), jnp.where(lane >= HALF, q, zero)], axis=0)


def _pair_softmax_pv(scores, values, m_q):
    top = functools.reduce(jnp.maximum, [jnp.max(s, axis=0, keepdims=True) for s in scores])
    ps = [jnp.exp2(s - top) for s in scores]
    inv = 1.0 / functools.reduce(jnp.add, [jnp.sum(p, axis=0, keepdims=True) for p in ps])
    out = functools.reduce(jnp.add, [_dot_tn(p.astype(BF16), v) for p, v in zip(ps, values)])
    inv_rows = jnp.concatenate(
        [jnp.broadcast_to(inv[:, c:c + LANES], (LANES, LANES)).T for c in range(0, 2 * m_q, LANES)], axis=0)
    out = out * inv_rows
    return jnp.where(_lane_iota((m_q, LANES)) < HALF, out[:m_q], out[m_q:])


def _na_kernel(q_ref, k_ref, v_ref, qc_ref, kc_ref, vc_ref, bias_ref, ox_ref, *rest):
    oc_ref = rest[0] if len(rest) == 3 else None
    s_even, s_odd = rest[-2:]
    n = q_ref.shape[0]
    rows = n // GRID_W
    win = NA_WIN_ROWS * GRID_W
    n_c = kc_ref.shape[0]

    def window_start(r):
        rs = jnp.clip(r - NA_WIN_ROWS // 2, 0, rows - NA_WIN_ROWS)
        return rs, pl.multiple_of(rs * GRID_W, GRID_W)

    def score_row(r, s_ref):
        rs, kstart = window_start(r)
        q_bd = _pair_block_diag(q_ref[pl.ds(pl.multiple_of(r * GRID_W, GRID_W), GRID_W), :])
        dr0 = rs - r + NA_WIN_ROWS - 1
        bias = jnp.concatenate([bias_ref[dr0 + a] for a in range(NA_WIN_ROWS)], axis=0)
        s_ref[0:win, :] = _dot_nt(k_ref[pl.ds(kstart, win), :], q_bd) + bias
        s_ref[win:win + n_c, :] = _dot_nt(kc_ref[...], q_bd)

    def finish_row(r, s_ref):
        _, kstart = window_start(r)
        out = _pair_softmax_pv([s_ref[0:win, :], s_ref[win:win + n_c, :]],
                               [v_ref[pl.ds(kstart, win), :], vc_ref[...]], GRID_W)
        ox_ref[pl.ds(pl.multiple_of(r * GRID_W, GRID_W), GRID_W), :] = out.astype(ox_ref.dtype)

    score_row(0, s_even)

    def row_pair(i, carry):
        r = 2 * i
        score_row(r + 1, s_odd)
        finish_row(r, s_even)
        score_row(jnp.minimum(r + 2, rows - 1), s_even)
        finish_row(r + 1, s_odd)
        return carry

    lax.fori_loop(0, rows // 2, row_pair, 0, unroll=4)

    if oc_ref is not None:
        q_bd = _pair_block_diag(qc_ref[...])
        oc_ref[...] = _pair_softmax_pv([_dot_nt(kc_ref[...], q_bd)], [vc_ref[...]], n_c).astype(oc_ref.dtype)


def _na(qkv_x, qkv_c, bias_tab, layer, with_ctx):
    bsz, n_x, _ = qkv_x.shape
    n_c = qkv_c.shape[1]
    n_pairs = NA_HEADS // NA_PAIR
    blk = lambda n, off: pl.BlockSpec((None, n, LANES), lambda b, p: (b, 0, off + p))
    out_shape = [jax.ShapeDtypeStruct((bsz, n_x, NA_WIDTH), BF16)]
    out_specs = [blk(n_x, 0)]
    if with_ctx:
        out_shape.append(jax.ShapeDtypeStruct((bsz, n_c, NA_WIDTH), BF16))
        out_specs.append(blk(n_c, 0))
    outs = pl.pallas_call(
        _na_kernel,
        out_shape=out_shape,
        grid=(bsz, n_pairs),
        in_specs=[blk(n_x, 0), blk(n_x, n_pairs), blk(n_x, 2 * n_pairs),
                  blk(n_c, 0), blk(n_c, n_pairs), blk(n_c, 2 * n_pairs),
                  pl.BlockSpec((None, None) + tuple(bias_tab.shape[2:]), lambda b, p: (layer, p, 0, 0, 0))],
        out_specs=out_specs,
        scratch_shapes=[pltpu.VMEM((NA_WIN_ROWS * GRID_W + n_c, NA_PAIR * GRID_W), F32)] * 2,
        compiler_params=_params(32, 2),
        name="na_mixer",
    )(qkv_x, qkv_x, qkv_x, qkv_c, qkv_c, qkv_c, bias_tab)
    return outs[0], (outs[1] if with_ctx else None)


def _na_bias_kernel(v_ref, o_ref):
    shape = (GRID_W, LANES)
    kc, lane = _row_iota(shape), _lane_iota(shape)
    j = lane % GRID_W
    off = kc - jnp.clip(j - NA_WIN_COLS // 2, 0, GRID_W - NA_WIN_COLS)
    for d in range(o_ref.shape[0]):
        t0 = pltpu.roll(jnp.broadcast_to(v_ref[0, d:d + 1, :], shape), 0, 1, stride=1, stride_axis=0)
        t1 = pltpu.roll(jnp.broadcast_to(v_ref[1, d:d + 1, :], shape), HALF, 1, stride=1, stride_axis=0)
        tile = jnp.where(lane < HALF, t0, t1)
        o_ref[d] = jnp.where(off >= 0, jnp.where(off < NA_WIN_COLS, tile, NEG_BIG), NEG_BIG)


def _na_bias_table(na_rpb):
    depth, heads, n_dr, n_dc = na_rpb.shape
    m = jnp.arange(LANES)
    t = jnp.where(m < HALF, m, m - LANES)
    rows = na_rpb[:, :, :, jnp.clip(NA_WIN_COLS - 1 - t, 0, n_dc - 1)] * LOG2_E
    rows = jnp.pad(rows, ((0, 0), (0, 0), (0, NA_DR - n_dr), (0, 0)))
    return pl.pallas_call(
        _na_bias_kernel,
        out_shape=jax.ShapeDtypeStruct((depth, heads // NA_PAIR, n_dr, GRID_W, LANES), F32),
        grid=(depth, heads // NA_PAIR),
        in_specs=[pl.BlockSpec((None, NA_PAIR, NA_DR, LANES), lambda l, p: (l, p, 0, 0))],
        out_specs=pl.BlockSpec((None, None, n_dr, GRID_W, LANES), lambda l, p: (l, p, 0, 0, 0)),
        compiler_params=_params(16, 2),
        name="na_bias_table",
    )(rows)


POOL_TILES = POOL_WIDTH // LANES


def _pool_kernel(*refs):
    u_tiles, (w_ref, scale_ref, o_ref) = refs[:POOL_TILES], refs[POOL_TILES:]
    t = SSD_CHUNK
    n = o_ref.shape[0]
    n_chunks = n // t

    def tile_windows(p):
        lo_g, hi_g = (p * LANES) // POOL_GROUP, ((p + 1) * LANES - 1) // POOL_GROUP
        split = (hi_g * POOL_GROUP) - p * LANES if hi_g != lo_g else None
        return POOL_WINDOWS[lo_g], POOL_WINDOWS[hi_g], split

    def chunk(base, first, last):
        pos = base + _row_iota((t, 1))
        tiles = []
        for p, u_ref in enumerate(u_tiles):
            w_lo, w_hi, split = tile_windows(p)
            x0 = u_ref[pl.ds(base, t), :]
            sums, acc, have = {}, x0, 1
            for w in POOL_WINDOWS:
                if w > w_hi:
                    break
                for d in list(range(-(w // 2), -(have // 2))) + list(range(have - have // 2, w // 2)):
                    acc = acc + _shifted_rows(u_ref, base, d, first, last)
                have = w
                cnt = (jnp.minimum(pos + w // 2, n) - jnp.maximum(pos - w // 2, 0)).astype(F32)
                sums[w] = acc / cnt
            if split is None:
                pooled = sums[w_lo]
            else:
                pooled = jnp.where(_lane_iota((t, LANES)) < split, sums[w_lo], sums[w_hi])
            tiles.append((pooled - x0).astype(BF16))
        mapped = _dot(jnp.concatenate(tiles, axis=1), w_ref[...])
        o_ref[pl.ds(base, t), :] = (mapped * scale_ref[...]).astype(o_ref.dtype)

    chunk(0, True, False)

    def body(c, carry):
        chunk(pl.multiple_of(c * t, t), False, False)
        return carry

    lax.fori_loop(1, n_chunks - 1, body, 0)
    chunk((n_chunks - 1) * t, False, True)


def _pool(u, w_bd, scale, layer):
    bsz, n, _ = u.shape
    return pl.pallas_call(
        _pool_kernel,
        out_shape=jax.ShapeDtypeStruct((bsz, n, POOL_WIDTH), BF16),
        grid=(bsz,),
        in_specs=[pl.BlockSpec((None, n, LANES), lambda b, p=p: (b, 0, p)) for p in range(POOL_TILES)]
        + [_layer_resident((POOL_WIDTH, POOL_WIDTH), layer), _layer_resident((1, POOL_WIDTH), layer)],
        out_specs=pl.BlockSpec((None, n, POOL_WIDTH), lambda b: (b, 0, 0)),
        compiler_params=_params(40),
        name="pool_mixer",
    )(*([u] * POOL_TILES), w_bd, scale)


FNET_TILE_X = 512
FNET_TILE_C = 128
FNET_TILE_PAD = 16
FNET_CHAN_ROWS = 1024


def _fnet_kernel(u_ref, cc_ref, sc_ref, pos_ref, flip_ref, o_ref, z_s, sum_s):
    n = u_ref.shape[0]
    tq = o_ref.shape[0]
    n_lo = sum_s.shape[0]
    t = pl.program_id(1)

    @pl.when(t == 0)
    def _():
        for r in range(0, n, min(n, FNET_CHAN_ROWS)):
            ub = u_ref[r:r + min(n, FNET_CHAN_ROWS), :].astype(BF16)
            z_s[r:r + ub.shape[0], :] = _dot(ub, cc_ref[...]).astype(BF16)
            z_s[n + r:n + r + ub.shape[0], :] = _dot(ub, sc_ref[...]).astype(BF16)

    scale = 1.0 / math.sqrt(n * FNET_HEAD_DIM)

    @pl.when(t < n_lo)
    def _():
        p = _dot(pos_ref[:, 0:n], z_s[0:n, :])
        q = _dot(pos_ref[:, n:2 * n], z_s[n:2 * n, :])
        o_ref[...] = ((p - q)[0:tq] * scale).astype(o_ref.dtype)
        sum_s[t] = ((p + q) * scale).astype(sum_s.dtype)

    @pl.when(t >= n_lo)
    def _():
        src = 2 * n_lo - 1 - t
        o_ref[...] = _dot(flip_ref[...], sum_s[src, 0:tq, :]).astype(o_ref.dtype)
        o_ref[0:1, :] = sum_s[src, tq:tq + 1, :]


def _fnet(u, chan_cos, chan_sin, pos_tab, flip):
    bsz, n, _ = u.shape
    n_lo, tq_pad, _ = pos_tab.shape
    tq = tq_pad - FNET_TILE_PAD
    return pl.pallas_call(
        _fnet_kernel,
        out_shape=jax.ShapeDtypeStruct((bsz, n, FNET_WIDTH), BF16),
        grid=(bsz, 2 * n_lo),
        in_specs=[pl.BlockSpec((None, n, FNET_WIDTH), lambda b, j: (b, 0, 0)),
                  _resident((FNET_WIDTH, FNET_WIDTH)),
                  _resident((FNET_WIDTH, FNET_WIDTH)),
                  pl.BlockSpec((None, tq_pad, 2 * n), lambda b, j: (jnp.minimum(j, n_lo - 1), 0, 0)),
                  _resident((tq, tq))],
        out_specs=pl.BlockSpec((None, tq, FNET_WIDTH), lambda b, j: (b, j, 0)),
        scratch_shapes=[pltpu.VMEM((2 * n, FNET_WIDTH), BF16), pltpu.VMEM((n_lo, tq_pad, FNET_WIDTH), BF16)],
        compiler_params=_params(48, 2),
        name="fnet_mixer",
    )(u, chan_cos, chan_sin, pos_tab, flip)


def _dft_cos_sin(rows, n):
    m = jnp.arange(n, dtype=jnp.int32)
    ang = ((rows[:, None] * m[None, :]) % n).astype(F32) * (2.0 * math.pi / n)
    return jnp.cos(ang), jnp.sin(ang)


def _fnet_tables(n, tq):
    n_lo = n // (2 * tq)
    rows = (jnp.arange(n_lo, dtype=jnp.int32)[:, None] * tq + jnp.arange(tq + FNET_TILE_PAD, dtype=jnp.int32)[None, :])
    c, s = _dft_cos_sin(rows.reshape(-1), n)
    tab = jnp.concatenate([c, s], axis=1).astype(BF16).reshape(n_lo, tq + FNET_TILE_PAD, 2 * n)
    i = jnp.arange(tq, dtype=jnp.int32)[:, None]
    src = jnp.arange(tq, dtype=jnp.int32)[None, :]
    return tab, (src == tq - i).astype(BF16)


def _fnet_channel_tables():
    c, s = _dft_cos_sin(jnp.arange(FNET_HEAD_DIM, dtype=jnp.int32), FNET_HEAD_DIM)
    eye = jnp.eye(FNET_HEADS, dtype=F32)
    return jnp.kron(eye, c).astype(BF16), jnp.kron(eye, s).astype(BF16)


def _merge_kernel(x_ref, mod_ref, ssd_ref, na_ref, pool_ref, fnet_ref,
                  wg_ref, bg_ref, wssd_ref, wna_ref, wpool_ref, wfnet_ref, wmix_ref, g_ref, b_ref, o_ref):
    x = x_ref[...]
    h = (_ln(x) * (1.0 + mod_ref[1:2, :]) + mod_ref[0:1, :]).astype(BF16)
    merged = None
    branches = ((ssd_ref, wssd_ref), (na_ref, wna_ref), (pool_ref, wpool_ref), (fnet_ref, wfnet_ref))
    for i, (y_ref, w_ref) in enumerate(branches):
        cols = slice(i * D_MODEL, (i + 1) * D_MODEL)
        gate = jax.nn.sigmoid(_dot(h, wg_ref[:, cols]) + bg_ref[:, cols])
        term = gate * _dot(y_ref[...], w_ref[...])
        merged = term if merged is None else merged + term
    mix = _dot(merged.astype(BF16), wmix_ref[...])
    y = _ln(DEEPNORM_ALPHA * x + mod_ref[2:3, :] * mix)
    o_ref[...] = y * g_ref[...] + b_ref[...]


def _merge(x2d, mod, tiles_per_mod, layer, ssd, na, pool, fnet, weights, tm):
    m = x2d.shape[0]
    mod_map = (lambda i: (0, 0, 0)) if tiles_per_mod is None else (lambda i: (i // tiles_per_mod, 0, 0))
    tok = lambda n: pl.BlockSpec((tm, n), lambda i: (i, 0))
    wg, bg, wssd, wna, wpool, wfnet, wmix, ln_g, ln_b = weights
    in_specs = [tok(D_MODEL), pl.BlockSpec((None, 8, D_MODEL), mod_map),
                tok(SSD_D_INNER), tok(NA_WIDTH), tok(POOL_WIDTH), tok(FNET_WIDTH),
                _layer_resident((D_MODEL, N_BRANCH * D_MODEL), layer),
                _layer_resident((1, N_BRANCH * D_MODEL), layer),
                _layer_resident((SSD_D_INNER, D_MODEL), layer),
                _layer_resident((NA_WIDTH, D_MODEL), layer),
                _layer_resident((POOL_WIDTH, D_MODEL), layer),
                _layer_resident((FNET_WIDTH, D_MODEL), layer),
                _layer_resident((D_MODEL, D_MODEL), layer),
                _layer_resident((1, D_MODEL), layer),
                _layer_resident((1, D_MODEL), layer)]
    return pl.pallas_call(
        _merge_kernel,
        out_shape=jax.ShapeDtypeStruct((m, D_MODEL), F32),
        grid=(m // tm,),
        in_specs=in_specs,
        out_specs=tok(D_MODEL),
        compiler_params=_params(58),
        name="merge_mix",
    )(x2d, mod, ssd, na, pool, fnet, wg, bg, wssd, wna, wpool, wfnet, wmix, ln_g, ln_b)


MLP_CHUNK = 1024


def _mlp_kernel(x_ref, mod_ref, wup_ref, bup_ref, wdn_ref, bdn_ref, g_ref, b_ref, o_ref):
    x = x_ref[...]
    h = (_ln(x) * (1.0 + mod_ref[4:5, :]) + mod_ref[3:4, :]).astype(BF16)
    acc = None
    for c in range(D_FF // MLP_CHUNK):
        cols = slice(c * MLP_CHUNK, (c + 1) * MLP_CHUNK)
        a = jnp.maximum(_dot(h, wup_ref[:, cols]) + bup_ref[:, cols], 0.0)
        part = _dot((a * a).astype(BF16), wdn_ref[cols, :])
        acc = part if acc is None else acc + part
    y = _ln(DEEPNORM_ALPHA * x + mod_ref[5:6, :] * (acc + bdn_ref[...]))
    o_ref[...] = y * g_ref[...] + b_ref[...]


def _mlp(x2d, mod, tiles_per_mod, layer, weights, tm):
    m = x2d.shape[0]
    mod_map = (lambda i: (0, 0, 0)) if tiles_per_mod is None else (lambda i: (i // tiles_per_mod, 0, 0))
    tok = pl.BlockSpec((tm, D_MODEL), lambda i: (i, 0))
    wup, bup, wdn, bdn, ln_g, ln_b = weights
    in_specs = [tok, pl.BlockSpec((None, 8, D_MODEL), mod_map),
                _layer_resident((D_MODEL, D_FF), layer), _layer_resident((1, D_FF), layer),
                _layer_resident((D_FF, D_MODEL), layer), _layer_resident((1, D_MODEL), layer),
                _layer_resident((1, D_MODEL), layer), _layer_resident((1, D_MODEL), layer)]
    return pl.pallas_call(
        _mlp_kernel,
        out_shape=jax.ShapeDtypeStruct((m, D_MODEL), F32),
        grid=(m // tm,),
        in_specs=in_specs,
        out_specs=tok,
        compiler_params=_params(58),
        name="mlp",
    )(x2d, mod, wup, bup, wdn, bdn, ln_g, ln_b)


def _rope_tables(n):
    pos = jnp.arange(n)
    quarter = SSD_STATE // 4
    inv_freq = ROPE_THETA ** (-jnp.arange(quarter, dtype=F32) / quarter)
    row = (pos // GRID_W).astype(F32)[:, None] * inv_freq
    col = (pos % GRID_W).astype(F32)[:, None] * inv_freq
    ang = jnp.concatenate([row, col], -1)
    cos, sin = jnp.cos(ang), jnp.sin(ang)
    return jnp.concatenate([cos, cos], -1), jnp.concatenate([-sin, sin], -1)


def _dt_lane_layout(a):
    lead = a.shape[:-2]
    a = a.reshape(lead + (2, SSD_GROUPS, HEADS_PER_GROUP))
    a = jnp.swapaxes(a, -3, -2).reshape(lead + (2 * SSD_HEADS,))
    return jnp.pad(a, [(0, 0)] * len(lead) + [(0, DT_PAD - 2 * SSD_HEADS)])


def _block_diag(w):
    l, g, c, _ = w.shape
    eye = jnp.eye(g, dtype=w.dtype)
    return jnp.einsum("lgcd,gh->lgchd", w, eye).reshape(l, g * c, g * c)


def kernel(x, c, ctx, c_ctx, ada_w, ada_b, in_w, in_b, ssd_conv_w, ssd_conv_b, ssd_dt_bias, ssd_a_log, ssd_d,
           ssd_norm_w, ssd_out_w, na_rpb, na_out_w, pool_w, pool_scale, pool_out_w, fnet_out_w, mix_out_w,
           ln1_g, ln1_b, mlp_up_w, mlp_up_b, mlp_down_w, mlp_down_b, ln2_g, ln2_b):
    bsz, n_x, d = x.shape
    n_c = ctx.shape[1]
    depth = in_w.shape[0]

    offs = [0]
    for wdt in (SSD_D_INNER, SSD_XBC, 2 * SSD_HEADS, 3 * NA_WIDTH, POOL_WIDTH, FNET_WIDTH, N_BRANCH * D_MODEL):
        offs.append(offs[-1] + wdt)

    def in_cols(a, i):
        return a[..., offs[i]:offs[i + 1]]

    def split_in(a):
        dt_cols = in_cols(a, 2)
        dt_cols = _dt_lane_layout(dt_cols.reshape(dt_cols.shape[:-1] + (2, SSD_HEADS)))
        qkv = in_cols(a, 3)
        q_scale = jnp.concatenate([jnp.full((NA_WIDTH,), NA_HEAD_DIM ** -0.5 * LOG2_E, F32), jnp.ones((2 * NA_WIDTH,), F32)])
        return [in_cols(a, 0), in_cols(a, 1), dt_cols, qkv * q_scale, in_cols(a, 4), in_cols(a, 5)], in_cols(a, 6)

    w_parts, w_gate = split_in(in_w)
    b_parts, b_gate = split_in(in_b[:, None, :])
    w_parts = [w.astype(BF16) for w in w_parts]
    w_gate = w_gate.astype(BF16)

    conv_w = jnp.pad(ssd_conv_w, ((0, 0), (0, 8 - SSD_CONV), (0, 0)))
    conv_b = ssd_conv_b[:, None, :]
    dt_bias = _dt_lane_layout(ssd_dt_bias)[:, None, :]
    a_log = _dt_lane_layout(ssd_a_log)[:, None, :]
    d_ch = jnp.repeat(ssd_d, SSD_HEADDIM, axis=-1)[:, None, :]
    norm_w = ssd_norm_w[:, None, :]
    cos2, sin2 = _rope_tables(n_x)

    bias_tab = _na_bias_table(na_rpb)
    pool_bd = _block_diag(pool_w).astype(BF16)
    pool_sc = pool_scale[:, None, :]
    chan_cos, chan_sin = _fnet_channel_tables()
    pos_tab_x, flip_x = _fnet_tables(n_x, FNET_TILE_X)
    pos_tab_c, flip_c = _fnet_tables(n_c, FNET_TILE_C)

    merge_w = [w_gate, b_gate, ssd_out_w.astype(BF16), na_out_w.astype(BF16), pool_out_w.astype(BF16),
               fnet_out_w.astype(BF16), mix_out_w.astype(BF16), ln1_g[:, None, :], ln1_b[:, None, :]]
    mlp_w = [mlp_up_w.astype(BF16), mlp_up_b[:, None, :], mlp_down_w.astype(BF16), mlp_down_b[:, None, :],
             ln2_g[:, None, :], ln2_b[:, None, :]]

    rows = ((bsz + 1 + 7) // 8) * 8
    c_all = jnp.zeros((rows, d), F32).at[:bsz].set(c).at[bsz].set(c_ctx)
    mods = _ada_all_layers(c_all, ada_w, ada_b).reshape(depth, rows, 6, d)
    mods = jnp.pad(mods, ((0, 0), (0, 0), (0, 2), (0, 0)))

    tm = TOKEN_TILE
    tm_c = min(tm, bsz * n_c)
    assert n_x % tm == 0 and (bsz * n_c) % tm_c == 0
    tiles_x = n_x // tm
    xf = x.reshape(bsz * n_x, d)
    cf = ctx.reshape(bsz * n_c, d)
    for i in range(depth):
        need_ctx = i < depth - 1
        mod_x, mod_c = mods[i, :bsz], mods[i, bsz:bsz + 1]
        z_x, xbc_x, dt_x, qkv_x, pool_x, fnet_x = _inproj(xf, mod_x, tiles_x, i, w_parts, b_parts, tm)
        z_c, xbc_c, dt_c, qkv_c, pool_c, fnet_c = _inproj(cf, mod_c, None, i, w_parts, b_parts, tm_c)
        r3 = lambda a, n: a.reshape(bsz, n, a.shape[-1])
        ssd_x, ssd_c = _ssd(r3(xbc_x, n_x), r3(dt_x, n_x), r3(z_x, n_x), r3(xbc_c, n_c), r3(dt_c, n_c), r3(z_c, n_c),
                            conv_w, conv_b, dt_bias, a_log, d_ch, norm_w, cos2, sin2, i)
        na_x, na_c = _na(r3(qkv_x, n_x), r3(qkv_c, n_c), bias_tab, i, need_ctx)
        pm_x = _pool(r3(pool_x, n_x), pool_bd, pool_sc, i)
        fm_x = _fnet(r3(fnet_x, n_x), chan_cos, chan_sin, pos_tab_x, flip_x)
        f2 = lambda a: a.reshape(-1, a.shape[-1])
        x1 = _merge(xf, mod_x, tiles_x, i, f2(ssd_x), f2(na_x), f2(pm_x), f2(fm_x), merge_w, tm)
        xf = _mlp(x1, mod_x, tiles_x, i, mlp_w, tm)
        if need_ctx:
            pm_c = _pool(r3(pool_c, n_c), pool_bd, pool_sc, i)
            fm_c = _fnet(r3(fnet_c, n_c), chan_cos, chan_sin, pos_tab_c, flip_c)
            c1 = _merge(cf, mod_c, None, i, f2(ssd_c), f2(na_c), f2(pm_c), f2(fm_c), merge_w, tm_c)
            cf = _mlp(c1, mod_c, None, i, mlp_w, tm_c)
    return xf.reshape(bsz, n_x, d)
```

```python
import functools
import math

import jax
import jax.numpy as jnp
from jax import lax
from jax.experimental import pallas as pl
from jax.experimental.pallas import tpu as pltpu

F32 = jnp.float32
BF16 = jnp.bfloat16

D_MODEL = 1024
DEPTH = 4
GRID_W = 64

SSD_HEADS = 16
SSD_HEADDIM = 64
SSD_D_INNER = SSD_HEADS * SSD_HEADDIM
SSD_GROUPS = 4
SSD_STATE = 128
SSD_CONV = 5
SSD_CHUNK = 128
SSD_XBC = SSD_D_INNER + 2 * SSD_GROUPS * SSD_STATE
HEADS_PER_GROUP = SSD_HEADS // SSD_GROUPS
GROUP_WIDTH = HEADS_PER_GROUP * SSD_HEADDIM

NA_HEADS = 16
NA_HEAD_DIM = 64
NA_WIDTH = NA_HEADS * NA_HEAD_DIM
NA_WIN_ROWS = 8
NA_WIN_COLS = 16

POOL_WINDOWS = (2, 4, 8, 16)
POOL_GROUP = 192
POOL_WIDTH = 4 * POOL_GROUP

FNET_HEADS = 4
FNET_HEAD_DIM = 192
FNET_WIDTH = FNET_HEADS * FNET_HEAD_DIM

N_BRANCH = 4
D_FF = 4 * D_MODEL
ROPE_THETA = 10000.0
DEEPNORM_ALPHA = (2 * DEPTH) ** 0.25
LN_EPS = 1e-6
RMS_EPS = 1e-5

LANES = 128
HALF = LANES // 2
DT_PAD = LANES
TOKEN_TILE = 512
NEG_BIG = -1e30
LOG2_E = 1.4426950408889634
MIB = 1024 * 1024


def _resident(shape):
    nd = len(shape)
    return pl.BlockSpec(shape, lambda *_: (0,) * nd, pipeline_mode=pl.Buffered(1))


def _layer_resident(shape, layer):
    nd = len(shape)
    return pl.BlockSpec((None,) + tuple(shape), lambda *_: (layer,) + (0,) * nd, pipeline_mode=pl.Buffered(1))


def _params(vmem_mib, n_axes=1):
    return pltpu.CompilerParams(dimension_semantics=("arbitrary",) * n_axes, vmem_limit_bytes=vmem_mib * MIB)


def _ln(x):
    mu = jnp.mean(x, axis=-1, keepdims=True)
    xc = x - mu
    var = jnp.mean(xc * xc, axis=-1, keepdims=True)
    return xc * lax.rsqrt(var + LN_EPS)


def _silu(x):
    return x * jax.nn.sigmoid(x)


def _dot(a, b):
    return jnp.dot(a, b, preferred_element_type=F32)


def _dot_nt(a, b):
    return lax.dot_general(a, b, (((1,), (1,)), ((), ())), preferred_element_type=F32)


def _ada_kernel(c_ref, w_ref, b_ref, o_ref):
    a = _silu(c_ref[...]).astype(BF16)
    o_ref[...] = _dot(a, w_ref[...].astype(BF16)) + b_ref[...]


def _ada_all_layers(c_all, ada_w, ada_b):
    rows = c_all.shape[0]
    return pl.pallas_call(
        _ada_kernel,
        out_shape=jax.ShapeDtypeStruct((DEPTH, rows, 6 * D_MODEL), F32),
        grid=(DEPTH, 6),
        in_specs=[
            _resident((rows, D_MODEL)),
            pl.BlockSpec((None, D_MODEL, D_MODEL), lambda l, j: (l, 0, j)),
            pl.BlockSpec((None, 1, D_MODEL), lambda l, j: (l, 0, j)),
        ],
        out_specs=pl.BlockSpec((None, rows, D_MODEL), lambda l, j: (l, 0, j)),
        compiler_params=_params(32, 2),
        name="ada_mod",
    )(c_all, ada_w, ada_b.reshape(DEPTH, 1, 6 * D_MODEL))


IN_SPLITS = (SSD_D_INNER, SSD_XBC, DT_PAD, 3 * NA_WIDTH, POOL_WIDTH, FNET_WIDTH)
IN_DTYPES = (F32, F32, F32, BF16, F32, F32)


def _inproj_kernel(x_ref, mod_ref, *refs):
    n = len(IN_SPLITS)
    w_refs, b_refs, o_refs = refs[:n], refs[n:2 * n], refs[2 * n:]
    h = (_ln(x_ref[...]) * (1.0 + mod_ref[1:2, :]) + mod_ref[0:1, :]).astype(BF16)
    for w_ref, b_ref, o_ref in zip(w_refs, b_refs, o_refs):
        o_ref[...] = (_dot(h, w_ref[...]) + b_ref[...]).astype(o_ref.dtype)


def _inproj(x2d, mod, tiles_per_mod, layer, weights, biases, tm):
    m = x2d.shape[0]
    if tiles_per_mod is None:
        mod_map = lambda i: (0, 0, 0)
    else:
        mod_map = lambda i: (i // tiles_per_mod, 0, 0)
    in_specs = [pl.BlockSpec((tm, D_MODEL), lambda i: (i, 0)), pl.BlockSpec((None, 8, D_MODEL), mod_map)]
    in_specs += [_layer_resident((D_MODEL, n), layer) for n in IN_SPLITS]
    in_specs += [_layer_resident((1, n), layer) for n in IN_SPLITS]
    return pl.pallas_call(
        _inproj_kernel,
        out_shape=[jax.ShapeDtypeStruct((m, n), dt) for n, dt in zip(IN_SPLITS, IN_DTYPES)],
        grid=(m // tm,),
        in_specs=in_specs,
        out_specs=[pl.BlockSpec((tm, n), lambda i: (i, 0)) for n in IN_SPLITS],
        compiler_params=_params(58),
        name="in_proj",
    )(x2d, mod, *weights, *biases)


def _row_iota(shape):
    return lax.broadcasted_iota(jnp.int32, shape, 0)


def _lane_iota(shape):
    return lax.broadcasted_iota(jnp.int32, shape, 1)


def _shifted_rows(ref, base, d, first, last):
    t = SSD_CHUNK
    if d == 0 or not ((first and d < 0) or (last and d > 0)):
        return ref[pl.ds(base + d, t), :]
    x = ref[pl.ds(base, t), :]
    row = _row_iota(x.shape)
    if d < 0:
        return jnp.where(row >= -d, pltpu.roll(x, -d, 0), 0.0)
    return jnp.where(row < t - d, pltpu.roll(x, t - d, 0), 0.0)


def _split3(x):
    hi = x.astype(BF16)
    r1 = x - hi.astype(F32)
    mid = r1.astype(BF16)
    lo = (r1 - mid.astype(F32)).astype(BF16)
    return hi, mid, lo


def _ssd_kernel(xs0_x, xs1_x, bm_x, cm_x, dt_x, z_x, xs0_c, xs1_c, bm_c, cm_c, dt_c, z_c,
                wxs, wbm, wcm, bxs, bbm, bcm, dtb_ref, alog_ref, dch_ref, nw_ref, cos_ref, sin_ref,
                ox_ref, oc_ref,
                xs_s, b_s, bt_s, c_s, dtt_s, ac_s, act_s, y_s, ea_s, sc_s, cd_s, st_s):
    t = SSD_CHUNK
    n_x, n_c = bm_x.shape[0], bm_c.shape[0]
    nc_x, nc_c = n_x // t, n_c // t
    g = pl.program_id(1)
    half = _lane_iota((t, LANES)) < HALF
    half_row = _lane_iota((1, LANES)) < HALF
    li, si = _row_iota((t, t)), _lane_iota((t, t))
    tril = jnp.where(si <= li, 1.0, 0.0).astype(BF16)
    triu = jnp.where(si >= li, 1.0, 0.0).astype(BF16)
    is_fwd_lane = (_lane_iota((t, LANES)) % (2 * HEADS_PER_GROUP)) < HEADS_PER_GROUP
    a_row = -jnp.exp(alog_ref[...])

    def conv_silu(ref, w_ref, b_ref, base, lanes, first, last):
        acc = b_ref[:, lanes] + w_ref[0:1, lanes] * _shifted_rows(ref, base, -(SSD_CONV // 2), first, last)
        for k in range(1, SSD_CONV):
            acc = acc + w_ref[k:k + 1, lanes] * _shifted_rows(ref, base, k - SSD_CONV // 2, first, last)
        return _silu(acc)

    def prep_chunk(refs, base, u0, first, last, rope):
        xs_tiles, bm_r, cm_r = refs
        dst = pl.ds(u0 + base, t)
        for p, xs_r in enumerate(xs_tiles):
            lanes = slice(p * LANES, (p + 1) * LANES)
            xs_s[dst, lanes] = conv_silu(xs_r, wxs, bxs, base, lanes, first, last)
        bm = conv_silu(bm_r, wbm, bbm, base, slice(None), first, last)
        cm = conv_silu(cm_r, wcm, bcm, base, slice(None), first, last)
        if rope:
            cos2, sin2 = cos_ref[pl.ds(base, t), :], sin_ref[pl.ds(base, t), :]
            bm = bm * cos2 + pltpu.roll(bm, HALF, 1) * sin2
            cm = cm * cos2 + pltpu.roll(cm, HALF, 1) * sin2
        b_s[dst, :] = bm.astype(BF16)
        bt_s[dst, :] = bm.T
        c_s[dst, :] = cm.astype(BF16)

    def prep_dt(dt_r, n_chunks, u0):
        def body(c, carry):
            base = pl.multiple_of(c * t, t)
            dst = pl.ds(u0 + base, t)
            raw = dt_r[pl.ds(base, t), :] + dtb_ref[...]
            dt = jnp.maximum(raw, 0.0) + jnp.log1p(jnp.exp(-jnp.abs(raw)))
            dtt_s[dst, :] = dt.T
            parts = _split3(dt * a_row)
            prefix = _dot(tril, parts[0]) + _dot(tril, parts[1]) + _dot(tril, parts[2])
            suffix = _dot(triu, parts[0]) + _dot(triu, parts[1]) + _dot(triu, parts[2])
            acum = jnp.where(is_fwd_lane, prefix, suffix) * LOG2_E
            ac_s[dst, :] = acum
            act_s[dst, :] = acum.T
            return carry

        lax.fori_loop(0, n_chunks, body, 0, unroll=min(4, n_chunks))

    @pl.when(g == 0)
    def _():
        prep_dt(dt_c, nc_c, 0)
        prep_dt(dt_x, nc_x, n_c)

    def prep_seq(refs, n_chunks, u0, rope):
        prep_chunk(refs, 0, u0, True, False, rope)

        def body(c, carry):
            prep_chunk(refs, pl.multiple_of(c * t, t), u0, False, False, rope)
            return carry

        lax.fori_loop(1, n_chunks - 1, body, 0, unroll=2)
        prep_chunk(refs, (n_chunks - 1) * t, u0, False, True, rope)

    prep_seq(((xs0_c, xs1_c), bm_c, cm_c), nc_c, 0, False)
    prep_seq(((xs0_x, xs1_x), bm_x, cm_x), nc_x, n_c, True)

    n_chunks = nc_c + nc_x

    def local_chunk(u, carry):
        r0 = pl.multiple_of(u * t, t)
        rows = pl.ds(r0, t)
        cb = _dot_nt(c_s[rows, :], b_s[rows, :])
        acum, bt = ac_s[rows, :], bt_s[rows, :]
        lane = _lane_iota((t, LANES))
        x_stacked = []
        for p in range(GROUP_WIDTH // LANES):
            xb = xs_s[rows, p * LANES:(p + 1) * LANES].astype(BF16)
            zero = jnp.zeros_like(xb)
            x_stacked.append(jnp.concatenate([jnp.where(half, xb, zero), jnp.where(half, zero, xb)], axis=0))
        y_tiles = [None] * (GROUP_WIDTH // LANES)
        for dirn in range(2):
            tri = (si <= li) if dirn == 0 else (si >= li)
            end_lane = t - 1 if dirn == 0 else 0
            for p in range(GROUP_WIDTH // LANES):
                lanes = slice(p * LANES, (p + 1) * LANES)
                acols, aends, m_heads, bt_heads = [], [], [], []
                for jj in range(2):
                    ln = g * (2 * HEADS_PER_GROUP) + dirn * HEADS_PER_GROUP + 2 * p + jj
                    acol = jnp.sum(jnp.where(lane == ln, acum, 0.0), axis=1, keepdims=True)
                    arow = act_s[pl.ds(r0 + ln, 1), :]
                    dtrow = dtt_s[pl.ds(r0 + ln, 1), :]
                    aend = jnp.sum(jnp.where(_lane_iota((1, t)) == end_lane, arow, 0.0), axis=1, keepdims=True)
                    decay_dt = jnp.exp2(jnp.where(tri, acol - (arow - jnp.log2(dtrow)), NEG_BIG))
                    m_heads.append((cb * decay_dt).astype(BF16))
                    bt_heads.append((bt * (dtrow * jnp.exp2(aend - arow))).astype(BF16))
                    acols.append(acol)
                    aends.append(aend)
                y = _dot(jnp.concatenate(m_heads, axis=1), x_stacked[p])
                y_tiles[p] = y if y_tiles[p] is None else y_tiles[p] + y
                sc_s[dirn, rows, lanes] = _dot(jnp.concatenate(bt_heads, axis=1), x_stacked[p])
                ea_s[dirn, rows, lanes] = jnp.exp2(jnp.where(half, acols[0], acols[1]))
                cd = jnp.exp2(jnp.where(half_row, aends[0], aends[1]))
                cd_s[dirn, pl.ds(pl.multiple_of(u * 8, 8), 8), lanes] = jnp.broadcast_to(cd, (8, LANES))
        for p, y in enumerate(y_tiles):
            y_s[rows, p * LANES:(p + 1) * LANES] = y
        return carry

    lax.fori_loop(0, n_chunks, local_chunk, 0, unroll=3)

    def carry_step(i, carry):
        u_fwd = i
        u_bwd = jnp.where(i < nc_c, nc_c - 1 - i, n_chunks + nc_c - 1 - i)
        for dirn, u in ((0, u_fwd), (1, u_bwd)):
            rows = pl.ds(pl.multiple_of(u * t, t), t)
            st = st_s[dirn]
            cd = cd_s[dirn, pl.ds(pl.multiple_of(u * 8, 8), 1), :]
            st_s[dirn] = st * cd + sc_s[dirn, rows, :]
            sc_s[dirn, rows, :] = st
        return carry

    st_s[...] = jnp.zeros_like(st_s)
    lax.fori_loop(0, n_chunks, carry_step, 0)

    def finish(z_ref, o_ref, n_chunks, u0):
        def body(c, carry):
            base = pl.multiple_of(c * t, t)
            rows = pl.ds(u0 + base, t)
            zz = z_ref[pl.ds(base, t), :]
            y = y_s[rows, :] + dch_ref[...] * xs_s[rows, :]
            for dirn in range(2):
                y = y + _dot(c_s[rows, :], sc_s[dirn, rows, :].astype(BF16)) * ea_s[dirn, rows, :]
            y = y * _silu(zz)
            ms = jnp.mean(y * y, axis=-1, keepdims=True)
            o_ref[pl.ds(base, t), :] = (y * lax.rsqrt(ms + RMS_EPS) * nw_ref[...]).astype(o_ref.dtype)
            return carry

        lax.fori_loop(0, n_chunks, body, 0, unroll=min(4, n_chunks))

    finish(z_c, oc_ref, nc_c, 0)
    finish(z_x, ox_ref, nc_x, n_c)


def _ssd(xbc_x, dt_x, z_x, xbc_c, dt_c, z_c, conv_w, conv_b, dt_bias, a_log, d_ch, norm_w, cos2, sin2, layer):
    bsz, n_x, _ = xbc_x.shape
    n_c = xbc_c.shape[1]
    n_all = n_x + n_c
    gw, st = GROUP_WIDTH, SSD_STATE
    b_blk0 = SSD_D_INNER // st
    c_blk0 = b_blk0 + SSD_GROUPS

    def seq_specs(n):
        return [
            pl.BlockSpec((None, n, LANES), lambda b, g: (b, 0, 2 * g)),
            pl.BlockSpec((None, n, LANES), lambda b, g: (b, 0, 2 * g + 1)),
            pl.BlockSpec((None, n, st), lambda b, g: (b, 0, b_blk0 + g)),
            pl.BlockSpec((None, n, st), lambda b, g: (b, 0, c_blk0 + g)),
            pl.BlockSpec((None, n, DT_PAD), lambda b, g: (b, 0, 0)),
            pl.BlockSpec((None, n, gw), lambda b, g: (b, 0, g)),
        ]

    in_specs = seq_specs(n_x) + seq_specs(n_c) + [
        pl.BlockSpec((None, 8, gw), lambda b, g: (layer, 0, g)),
        pl.BlockSpec((None, 8, st), lambda b, g: (layer, 0, b_blk0 + g)),
        pl.BlockSpec((None, 8, st), lambda b, g: (layer, 0, c_blk0 + g)),
        pl.BlockSpec((None, 1, gw), lambda b, g: (layer, 0, g)),
        pl.BlockSpec((None, 1, st), lambda b, g: (layer, 0, b_blk0 + g)),
        pl.BlockSpec((None, 1, st), lambda b, g: (layer, 0, c_blk0 + g)),
        pl.BlockSpec((None, 1, DT_PAD), lambda b, g: (layer, 0, 0)),
        pl.BlockSpec((None, 1, DT_PAD), lambda b, g: (layer, 0, 0)),
        pl.BlockSpec((None, 1, gw), lambda b, g: (layer, 0, g)),
        pl.BlockSpec((None, 1, gw), lambda b, g: (layer, 0, g)),
        _resident((n_x, st)),
        _resident((n_x, st)),
    ]
    out_shape = [jax.ShapeDtypeStruct((bsz, n_x, SSD_D_INNER), BF16),
                 jax.ShapeDtypeStruct((bsz, n_c, SSD_D_INNER), BF16)]
    out_specs = [pl.BlockSpec((None, n_x, gw), lambda b, g: (b, 0, g)),
                 pl.BlockSpec((None, n_c, gw), lambda b, g: (b, 0, g))]
    scratch = [
        pltpu.VMEM((n_all, gw), F32),
        pltpu.VMEM((n_all, st), BF16),
        pltpu.VMEM((n_all, st), F32),
        pltpu.VMEM((n_all, st), BF16),
        pltpu.VMEM((n_all, DT_PAD), F32),
        pltpu.VMEM((n_all, DT_PAD), F32),
        pltpu.VMEM((n_all, DT_PAD), F32),
        pltpu.VMEM((n_all, gw), F32),
        pltpu.VMEM((2, n_all, gw), F32),
        pltpu.VMEM((2, n_all, gw), F32),
        pltpu.VMEM((2, 8 * (n_all // SSD_CHUNK), gw), F32),
        pltpu.VMEM((2, st, gw), F32),
    ]
    return pl.pallas_call(
        _ssd_kernel,
        out_shape=out_shape,
        grid=(bsz, SSD_GROUPS),
        in_specs=in_specs,
        out_specs=out_specs,
        scratch_shapes=scratch,
        compiler_params=_params(52, 2),
        name="ssd_mixer",
    )(xbc_x, xbc_x, xbc_x, xbc_x, dt_x, z_x, xbc_c, xbc_c, xbc_c, xbc_c, dt_c, z_c,
      conv_w, conv_w, conv_w, conv_b, conv_b, conv_b, dt_bias, a_log, d_ch, norm_w, cos2, sin2)


NA_PAIR = 2
NA_DR = 2 * NA_WIN_ROWS


def _dot_tn(a, b):
    return lax.dot_general(a, b, (((0,), (0,)), ((), ())), preferred_element_type=F32)


def _pair_block_diag(q):
    lane = _lane_iota(q.shape)
    zero = jnp.zeros_like(q)
    return jnp.concatenate([jnp.where(lane < HALF, q, zero), jnp.where(lane >= HALF, q, zero)], axis=0)


def _pair_softmax_pv(scores, values, m_q):
    top = functools.reduce(jnp.maximum, [jnp.max(s, axis=0, keepdims=True) for s in scores])
    ps = [jnp.exp2(s - top) for s in scores]
    inv = 1.0 / functools.reduce(jnp.add, [jnp.sum(p, axis=0, keepdims=True) for p in ps])
    out = functools.reduce(jnp.add, [_dot_tn(p.astype(BF16), v) for p, v in zip(ps, values)])
    inv_rows = jnp.concatenate(
        [jnp.broadcast_to(inv[:, c:c + LANES], (LANES, LANES)).T for c in range(0, 2 * m_q, LANES)], axis=0)
    out = out * inv_rows
    return jnp.where(_lane_iota((m_q, LANES)) < HALF, out[:m_q], out[m_q:])


def _na_kernel(q_ref, k_ref, v_ref, qc_ref, kc_ref, vc_ref, bias_ref, ox_ref, *rest):
    oc_ref = rest[0] if len(rest) == 3 else None
    s_even, s_odd = rest[-2:]
    n = q_ref.shape[0]
    rows = n // GRID_W
    win = NA_WIN_ROWS * GRID_W
    n_c = kc_ref.shape[0]

    def window_start(r):
        rs = jnp.clip(r - NA_WIN_ROWS // 2, 0, rows - NA_WIN_ROWS)
        return rs, pl.multiple_of(rs * GRID_W, GRID_W)

    def score_row(r, s_ref):
        rs, kstart = window_start(r)
        q_bd = _pair_block_diag(q_ref[pl.ds(pl.multiple_of(r * GRID_W, GRID_W), GRID_W), :])
        dr0 = rs - r + NA_WIN_ROWS - 1
        bias = jnp.concatenate([bias_ref[dr0 + a] for a in range(NA_WIN_ROWS)], axis=0)
        s_ref[0:win, :] = _dot_nt(k_ref[pl.ds(kstart, win), :], q_bd) + bias
        s_ref[win:win + n_c, :] = _dot_nt(kc_ref[...], q_bd)

    def finish_row(r, s_ref):
        _, kstart = window_start(r)
        out = _pair_softmax_pv([s_ref[0:win, :], s_ref[win:win + n_c, :]],
                               [v_ref[pl.ds(kstart, win), :], vc_ref[...]], GRID_W)
        ox_ref[pl.ds(pl.multiple_of(r * GRID_W, GRID_W), GRID_W), :] = out.astype(ox_ref.dtype)

    score_row(0, s_even)

    def row_pair(i, carry):
        r = 2 * i
        score_row(r + 1, s_odd)
        finish_row(r, s_even)
        score_row(jnp.minimum(r + 2, rows - 1), s_even)
        finish_row(r + 1, s_odd)
        return carry

    lax.fori_loop(0, rows // 2, row_pair, 0, unroll=8)

    if oc_ref is not None:
        q_bd = _pair_block_diag(qc_ref[...])
        oc_ref[...] = _pair_softmax_pv([_dot_nt(kc_ref[...], q_bd)], [vc_ref[...]], n_c).astype(oc_ref.dtype)


def _na(qkv_x, qkv_c, bias_tab, layer, with_ctx):
    bsz, n_x, _ = qkv_x.shape
    n_c = qkv_c.shape[1]
    n_pairs = NA_HEADS // NA_PAIR
    blk = lambda n, off: pl.BlockSpec((None, n, LANES), lambda b, p: (b, 0, off + p))
    out_shape = [jax.ShapeDtypeStruct((bsz, n_x, NA_WIDTH), BF16)]
    out_specs = [blk(n_x, 0)]
    if with_ctx:
        out_shape.append(jax.ShapeDtypeStruct((bsz, n_c, NA_WIDTH), BF16))
        out_specs.append(blk(n_c, 0))
    outs = pl.pallas_call(
        _na_kernel,
        out_shape=out_shape,
        grid=(bsz, n_pairs),
        in_specs=[blk(n_x, 0), blk(n_x, n_pairs), blk(n_x, 2 * n_pairs),
                  blk(n_c, 0), blk(n_c, n_pairs), blk(n_c, 2 * n_pairs),
                  pl.BlockSpec((None, None) + tuple(bias_tab.shape[2:]), lambda b, p: (layer, p, 0, 0, 0))],
        out_specs=out_specs,
        scratch_shapes=[pltpu.VMEM((NA_WIN_ROWS * GRID_W + n_c, NA_PAIR * GRID_W), F32)] * 2,
        compiler_params=_params(32, 2),
        name="na_mixer",
    )(qkv_x, qkv_x, qkv_x, qkv_c, qkv_c, qkv_c, bias_tab)
    return outs[0], (outs[1] if with_ctx else None)


def _na_bias_kernel(v_ref, o_ref):
    shape = (GRID_W, LANES)
    kc, lane = _row_iota(shape), _lane_iota(shape)
    j = lane % GRID_W
    off = kc - jnp.clip(j - NA_WIN_COLS // 2, 0, GRID_W - NA_WIN_COLS)
    for d in range(o_ref.shape[0]):
        t0 = pltpu.roll(jnp.broadcast_to(v_ref[0, d:d + 1, :], shape), 0, 1, stride=1, stride_axis=0)
        t1 = pltpu.roll(jnp.broadcast_to(v_ref[1, d:d + 1, :], shape), HALF, 1, stride=1, stride_axis=0)
        tile = jnp.where(lane < HALF, t0, t1)
        o_ref[d] = jnp.where(off >= 0, jnp.where(off < NA_WIN_COLS, tile, NEG_BIG), NEG_BIG)


def _na_bias_table(na_rpb):
    depth, heads, n_dr, n_dc = na_rpb.shape
    m = jnp.arange(LANES)
    t = jnp.where(m < HALF, m, m - LANES)
    rows = na_rpb[:, :, :, jnp.clip(NA_WIN_COLS - 1 - t, 0, n_dc - 1)] * LOG2_E
    rows = jnp.pad(rows, ((0, 0), (0, 0), (0, NA_DR - n_dr), (0, 0)))
    return pl.pallas_call(
        _na_bias_kernel,
        out_shape=jax.ShapeDtypeStruct((depth, heads // NA_PAIR, n_dr, GRID_W, LANES), F32),
        grid=(depth, heads // NA_PAIR),
        in_specs=[pl.BlockSpec((None, NA_PAIR, NA_DR, LANES), lambda l, p: (l, p, 0, 0))],
        out_specs=pl.BlockSpec((None, None, n_dr, GRID_W, LANES), lambda l, p: (l, p, 0, 0, 0)),
        compiler_params=_params(16, 2),
        name="na_bias_table",
    )(rows)


POOL_TILES = POOL_WIDTH // LANES


def _pool_kernel(*refs):
    u_tiles, (w_ref, scale_ref, o_ref) = refs[:POOL_TILES], refs[POOL_TILES:]
    t = SSD_CHUNK
    n = o_ref.shape[0]
    n_chunks = n // t

    def tile_windows(p):
        lo_g, hi_g = (p * LANES) // POOL_GROUP, ((p + 1) * LANES - 1) // POOL_GROUP
        split = (hi_g * POOL_GROUP) - p * LANES if hi_g != lo_g else None
        return POOL_WINDOWS[lo_g], POOL_WINDOWS[hi_g], split

    def chunk(base, first, last):
        pos = base + _row_iota((t, 1))
        tiles = []
        for p, u_ref in enumerate(u_tiles):
            w_lo, w_hi, split = tile_windows(p)
            x0 = u_ref[pl.ds(base, t), :]
            sums, acc, have = {}, x0, 1
            for w in POOL_WINDOWS:
                if w > w_hi:
                    break
                for d in list(range(-(w // 2), -(have // 2))) + list(range(have - have // 2, w // 2)):
                    acc = acc + _shifted_rows(u_ref, base, d, first, last)
                have = w
                cnt = (jnp.minimum(pos + w // 2, n) - jnp.maximum(pos - w // 2, 0)).astype(F32)
                sums[w] = acc / cnt
            if split is None:
                pooled = sums[w_lo]
            else:
                pooled = jnp.where(_lane_iota((t, LANES)) < split, sums[w_lo], sums[w_hi])
            tiles.append((pooled - x0).astype(BF16))
        mapped = _dot(jnp.concatenate(tiles, axis=1), w_ref[...])
        o_ref[pl.ds(base, t), :] = (mapped * scale_ref[...]).astype(o_ref.dtype)

    chunk(0, True, False)

    def body(c, carry):
        chunk(pl.multiple_of(c * t, t), False, False)
        return carry

    lax.fori_loop(1, n_chunks - 1, body, 0)
    chunk((n_chunks - 1) * t, False, True)


def _pool(u, w_bd, scale, layer):
    bsz, n, _ = u.shape
    return pl.pallas_call(
        _pool_kernel,
        out_shape=jax.ShapeDtypeStruct((bsz, n, POOL_WIDTH), BF16),
        grid=(bsz,),
        in_specs=[pl.BlockSpec((None, n, LANES), lambda b, p=p: (b, 0, p)) for p in range(POOL_TILES)]
        + [_layer_resident((POOL_WIDTH, POOL_WIDTH), layer), _layer_resident((1, POOL_WIDTH), layer)],
        out_specs=pl.BlockSpec((None, n, POOL_WIDTH), lambda b: (b, 0, 0)),
        compiler_params=_params(40),
        name="pool_mixer",
    )(*([u] * POOL_TILES), w_bd, scale)


FNET_TILE_X = 512
FNET_TILE_C = 128
FNET_TILE_PAD = 16
FNET_CHAN_ROWS = 1024


def _fnet_kernel(u_ref, cc_ref, sc_ref, pos_ref, flip_ref, o_ref, z_s, sum_s):
    n = u_ref.shape[0]
    tq = o_ref.shape[0]
    n_lo = sum_s.shape[0]
    t = pl.program_id(1)

    @pl.when(t == 0)
    def _():
        for r in range(0, n, min(n, FNET_CHAN_ROWS)):
            ub = u_ref[r:r + min(n, FNET_CHAN_ROWS), :].astype(BF16)
            z_s[r:r + ub.shape[0], :] = _dot(ub, cc_ref[...]).astype(BF16)
            z_s[n + r:n + r + ub.shape[0], :] = _dot(ub, sc_ref[...]).astype(BF16)

    scale = 1.0 / math.sqrt(n * FNET_HEAD_DIM)

    @pl.when(t < n_lo)
    def _():
        p = _dot(pos_ref[:, 0:n], z_s[0:n, :])
        q = _dot(pos_ref[:, n:2 * n], z_s[n:2 * n, :])
        o_ref[...] = ((p - q)[0:tq] * scale).astype(o_ref.dtype)
        sum_s[t] = ((p + q) * scale).astype(sum_s.dtype)

    @pl.when(t >= n_lo)
    def _():
        src = 2 * n_lo - 1 - t
        o_ref[...] = _dot(flip_ref[...], sum_s[src, 0:tq, :]).astype(o_ref.dtype)
        o_ref[0:1, :] = sum_s[src, tq:tq + 1, :]


def _fnet(u, chan_cos, chan_sin, pos_tab, flip):
    bsz, n, _ = u.shape
    n_lo, tq_pad, _ = pos_tab.shape
    tq = tq_pad - FNET_TILE_PAD
    return pl.pallas_call(
        _fnet_kernel,
        out_shape=jax.ShapeDtypeStruct((bsz, n, FNET_WIDTH), BF16),
        grid=(bsz, 2 * n_lo),
        in_specs=[pl.BlockSpec((None, n, FNET_WIDTH), lambda b, j: (b, 0, 0)),
                  _resident((FNET_WIDTH, FNET_WIDTH)),
                  _resident((FNET_WIDTH, FNET_WIDTH)),
                  pl.BlockSpec((None, tq_pad, 2 * n), lambda b, j: (jnp.minimum(j, n_lo - 1), 0, 0)),
                  _resident((tq, tq))],
        out_specs=pl.BlockSpec((None, tq, FNET_WIDTH), lambda b, j: (b, j, 0)),
        scratch_shapes=[pltpu.VMEM((2 * n, FNET_WIDTH), BF16), pltpu.VMEM((n_lo, tq_pad, FNET_WIDTH), BF16)],
        compiler_params=_params(48, 2),
        name="fnet_mixer",
    )(u, chan_cos, chan_sin, pos_tab, flip)


def _dft_cos_sin(rows, n):
    m = jnp.arange(n, dtype=jnp.int32)
    ang = ((rows[:, None] * m[None, :]) % n).astype(F32) * (2.0 * math.pi / n)
    return jnp.cos(ang), jnp.sin(ang)


def _fnet_tables(n, tq):
    n_lo = n // (2 * tq)
    rows = (jnp.arange(n_lo, dtype=jnp.int32)[:, None] * tq + jnp.arange(tq + FNET_TILE_PAD, dtype=jnp.int32)[None, :])
    c, s = _dft_cos_sin(rows.reshape(-1), n)
    tab = jnp.concatenate([c, s], axis=1).astype(BF16).reshape(n_lo, tq + FNET_TILE_PAD, 2 * n)
    i = jnp.arange(tq, dtype=jnp.int32)[:, None]
    src = jnp.arange(tq, dtype=jnp.int32)[None, :]
    return tab, (src == tq - i).astype(BF16)


def _fnet_channel_tables():
    c, s = _dft_cos_sin(jnp.arange(FNET_HEAD_DIM, dtype=jnp.int32), FNET_HEAD_DIM)
    eye = jnp.eye(FNET_HEADS, dtype=F32)
    return jnp.kron(eye, c).astype(BF16), jnp.kron(eye, s).astype(BF16)


def _merge_kernel(x_ref, mod_ref, ssd_ref, na_ref, pool_ref, fnet_ref,
                  wg_ref, bg_ref, wssd_ref, wna_ref, wpool_ref, wfnet_ref, wmix_ref, g_ref, b_ref, o_ref):
    x = x_ref[...]
    h = (_ln(x) * (1.0 + mod_ref[1:2, :]) + mod_ref[0:1, :]).astype(BF16)
    merged = None
    branches = ((ssd_ref, wssd_ref), (na_ref, wna_ref), (pool_ref, wpool_ref), (fnet_ref, wfnet_ref))
    for i, (y_ref, w_ref) in enumerate(branches):
        cols = slice(i * D_MODEL, (i + 1) * D_MODEL)
        gate = jax.nn.sigmoid(_dot(h, wg_ref[:, cols]) + bg_ref[:, cols])
        term = gate * _dot(y_ref[...], w_ref[...])
        merged = term if merged is None else merged + term
    mix = _dot(merged.astype(BF16), wmix_ref[...])
    y = _ln(DEEPNORM_ALPHA * x + mod_ref[2:3, :] * mix)
    o_ref[...] = y * g_ref[...] + b_ref[...]


def _merge(x2d, mod, tiles_per_mod, layer, ssd, na, pool, fnet, weights, tm):
    m = x2d.shape[0]
    mod_map = (lambda i: (0, 0, 0)) if tiles_per_mod is None else (lambda i: (i // tiles_per_mod, 0, 0))
    tok = lambda n: pl.BlockSpec((tm, n), lambda i: (i, 0))
    wg, bg, wssd, wna, wpool, wfnet, wmix, ln_g, ln_b = weights
    in_specs = [tok(D_MODEL), pl.BlockSpec((None, 8, D_MODEL), mod_map),
                tok(SSD_D_INNER), tok(NA_WIDTH), tok(POOL_WIDTH), tok(FNET_WIDTH),
                _layer_resident((D_MODEL, N_BRANCH * D_MODEL), layer),
                _layer_resident((1, N_BRANCH * D_MODEL), layer),
                _layer_resident((SSD_D_INNER, D_MODEL), layer),
                _layer_resident((NA_WIDTH, D_MODEL), layer),
                _layer_resident((POOL_WIDTH, D_MODEL), layer),
                _layer_resident((FNET_WIDTH, D_MODEL), layer),
                _layer_resident((D_MODEL, D_MODEL), layer),
                _layer_resident((1, D_MODEL), layer),
                _layer_resident((1, D_MODEL), layer)]
    return pl.pallas_call(
        _merge_kernel,
        out_shape=jax.ShapeDtypeStruct((m, D_MODEL), F32),
        grid=(m // tm,),
        in_specs=in_specs,
        out_specs=tok(D_MODEL),
        compiler_params=_params(58),
        name="merge_mix",
    )(x2d, mod, ssd, na, pool, fnet, wg, bg, wssd, wna, wpool, wfnet, wmix, ln_g, ln_b)


MLP_CHUNK = 1024


def _mlp_kernel(x_ref, mod_ref, wup_ref, bup_ref, wdn_ref, bdn_ref, g_ref, b_ref, o_ref):
    x = x_ref[...]
    h = (_ln(x) * (1.0 + mod_ref[4:5, :]) + mod_ref[3:4, :]).astype(BF16)
    acc = None
    for c in range(D_FF // MLP_CHUNK):
        cols = slice(c * MLP_CHUNK, (c + 1) * MLP_CHUNK)
        a = jnp.maximum(_dot(h, wup_ref[:, cols]) + bup_ref[:, cols], 0.0)
        part = _dot((a * a).astype(BF16), wdn_ref[cols, :])
        acc = part if acc is None else acc + part
    y = _ln(DEEPNORM_ALPHA * x + mod_ref[5:6, :] * (acc + bdn_ref[...]))
    o_ref[...] = y * g_ref[...] + b_ref[...]


def _mlp(x2d, mod, tiles_per_mod, layer, weights, tm):
    m = x2d.shape[0]
    mod_map = (lambda i: (0, 0, 0)) if tiles_per_mod is None else (lambda i: (i // tiles_per_mod, 0, 0))
    tok = pl.BlockSpec((tm, D_MODEL), lambda i: (i, 0))
    wup, bup, wdn, bdn, ln_g, ln_b = weights
    in_specs = [tok, pl.BlockSpec((None, 8, D_MODEL), mod_map),
                _layer_resident((D_MODEL, D_FF), layer), _layer_resident((1, D_FF), layer),
                _layer_resident((D_FF, D_MODEL), layer), _layer_resident((1, D_MODEL), layer),
                _layer_resident((1, D_MODEL), layer), _layer_resident((1, D_MODEL), layer)]
    return pl.pallas_call(
        _mlp_kernel,
        out_shape=jax.ShapeDtypeStruct((m, D_MODEL), F32),
        grid=(m // tm,),
        in_specs=in_specs,
        out_specs=tok,
        compiler_params=_params(58),
        name="mlp",
    )(x2d, mod, wup, bup, wdn, bdn, ln_g, ln_b)


def _rope_tables(n):
    pos = jnp.arange(n)
    quarter = SSD_STATE // 4
    inv_freq = ROPE_THETA ** (-jnp.arange(quarter, dtype=F32) / quarter)
    row = (pos // GRID_W).astype(F32)[:, None] * inv_freq
    col = (pos % GRID_W).astype(F32)[:, None] * inv_freq
    ang = jnp.concatenate([row, col], -1)
    cos, sin = jnp.cos(ang), jnp.sin(ang)
    return jnp.concatenate([cos, cos], -1), jnp.concatenate([-sin, sin], -1)


def _dt_lane_layout(a):
    lead = a.shape[:-2]
    a = a.reshape(lead + (2, SSD_GROUPS, HEADS_PER_GROUP))
    a = jnp.swapaxes(a, -3, -2).reshape(lead + (2 * SSD_HEADS,))
    return jnp.pad(a, [(0, 0)] * len(lead) + [(0, DT_PAD - 2 * SSD_HEADS)])


def _block_diag(w):
    l, g, c, _ = w.shape
    eye = jnp.eye(g, dtype=w.dtype)
    return jnp.einsum("lgcd,gh->lgchd", w, eye).reshape(l, g * c, g * c)


def kernel(x, c, ctx, c_ctx, ada_w, ada_b, in_w, in_b, ssd_conv_w, ssd_conv_b, ssd_dt_bias, ssd_a_log, ssd_d,
           ssd_norm_w, ssd_out_w, na_rpb, na_out_w, pool_w, pool_scale, pool_out_w, fnet_out_w, mix_out_w,
           ln1_g, ln1_b, mlp_up_w, mlp_up_b, mlp_down_w, mlp_down_b, ln2_g, ln2_b):
    bsz, n_x, d = x.shape
    n_c = ctx.shape[1]
    depth = in_w.shape[0]

    offs = [0]
    for wdt in (SSD_D_INNER, SSD_XBC, 2 * SSD_HEADS, 3 * NA_WIDTH, POOL_WIDTH, FNET_WIDTH, N_BRANCH * D_MODEL):
        offs.append(offs[-1] + wdt)

    def in_cols(a, i):
        return a[..., offs[i]:offs[i + 1]]

    def split_in(a):
        dt_cols = in_cols(a, 2)
        dt_cols = _dt_lane_layout(dt_cols.reshape(dt_cols.shape[:-1] + (2, SSD_HEADS)))
        qkv = in_cols(a, 3)
        q_scale = jnp.concatenate([jnp.full((NA_WIDTH,), NA_HEAD_DIM ** -0.5 * LOG2_E, F32), jnp.ones((2 * NA_WIDTH,), F32)])
        return [in_cols(a, 0), in_cols(a, 1), dt_cols, qkv * q_scale, in_cols(a, 4), in_cols(a, 5)], in_cols(a, 6)

    w_parts, w_gate = split_in(in_w)
    b_parts, b_gate = split_in(in_b[:, None, :])
    w_parts = [w.astype(BF16) for w in w_parts]
    w_gate = w_gate.astype(BF16)

    conv_w = jnp.pad(ssd_conv_w, ((0, 0), (0, 8 - SSD_CONV), (0, 0)))
    conv_b = ssd_conv_b[:, None, :]
    dt_bias = _dt_lane_layout(ssd_dt_bias)[:, None, :]
    a_log = _dt_lane_layout(ssd_a_log)[:, None, :]
    d_ch = jnp.repeat(ssd_d, SSD_HEADDIM, axis=-1)[:, None, :]
    norm_w = ssd_norm_w[:, None, :]
    cos2, sin2 = _rope_tables(n_x)

    bias_tab = _na_bias_table(na_rpb)
    pool_bd = _block_diag(pool_w).astype(BF16)
    pool_sc = pool_scale[:, None, :]
    chan_cos, chan_sin = _fnet_channel_tables()
    pos_tab_x, flip_x = _fnet_tables(n_x, FNET_TILE_X)
    pos_tab_c, flip_c = _fnet_tables(n_c, FNET_TILE_C)

    merge_w = [w_gate, b_gate, ssd_out_w.astype(BF16), na_out_w.astype(BF16), pool_out_w.astype(BF16),
               fnet_out_w.astype(BF16), mix_out_w.astype(BF16), ln1_g[:, None, :], ln1_b[:, None, :]]
    mlp_w = [mlp_up_w.astype(BF16), mlp_up_b[:, None, :], mlp_down_w.astype(BF16), mlp_down_b[:, None, :],
             ln2_g[:, None, :], ln2_b[:, None, :]]

    rows = ((bsz + 1 + 7) // 8) * 8
    c_all = jnp.zeros((rows, d), F32).at[:bsz].set(c).at[bsz].set(c_ctx)
    mods = _ada_all_layers(c_all, ada_w, ada_b).reshape(depth, rows, 6, d)
    mods = jnp.pad(mods, ((0, 0), (0, 0), (0, 2), (0, 0)))

    tm = TOKEN_TILE
    tm_c = min(tm, bsz * n_c)
    assert n_x % tm == 0 and (bsz * n_c) % tm_c == 0
    tiles_x = n_x // tm
    xf = x.reshape(bsz * n_x, d)
    cf = ctx.reshape(bsz * n_c, d)
    for i in range(depth):
        need_ctx = i < depth - 1
        mod_x, mod_c = mods[i, :bsz], mods[i, bsz:bsz + 1]
        z_x, xbc_x, dt_x, qkv_x, pool_x, fnet_x = _inproj(xf, mod_x, tiles_x, i, w_parts, b_parts, tm)
        z_c, xbc_c, dt_c, qkv_c, pool_c, fnet_c = _inproj(cf, mod_c, None, i, w_parts, b_parts, tm_c)
        r3 = lambda a, n: a.reshape(bsz, n, a.shape[-1])
        ssd_x, ssd_c = _ssd(r3(xbc_x, n_x), r3(dt_x, n_x), r3(z_x, n_x), r3(xbc_c, n_c), r3(dt_c, n_c), r3(z_c, n_c),
                            conv_w, conv_b, dt_bias, a_log, d_ch, norm_w, cos2, sin2, i)
        na_x, na_c = _na(r3(qkv_x, n_x), r3(qkv_c, n_c), bias_tab, i, need_ctx)
        pm_x = _pool(r3(pool_x, n_x), pool_bd, pool_sc, i)
        fm_x = _fnet(r3(fnet_x, n_x), chan_cos, chan_sin, pos_tab_x, flip_x)
        f2 = lambda a: a.reshape(-1, a.shape[-1])
        x1 = _merge(xf, mod_x, tiles_x, i, f2(ssd_x), f2(na_x), f2(pm_x), f2(fm_x), merge_w, tm)
        xf = _mlp(x1, mod_x, tiles_x, i, mlp_w, tm)
        if need_ctx:
            pm_c = _pool(r3(pool_c, n_c), pool_bd, pool_sc, i)
            fm_c = _fnet(r3(fnet_c, n_c), chan_cos, chan_sin, pos_tab_c, flip_c)
            c1 = _merge(cf, mod_c, None, i, f2(ssd_c), f2(na_c), f2(pm_c), f2(fm_c), merge_w, tm_c)
            cf = _mlp(c1, mod_c, None, i, mlp_w, tm_c)
    return xf.reshape(bsz, n_x, d)
```

```python
import functools
import math

import jax
import jax.numpy as jnp
from jax import lax
from jax.experimental import pallas as pl
from jax.experimental.pallas import tpu as pltpu

F32 = jnp.float32
BF16 = jnp.bfloat16

D_MODEL = 1024
DEPTH = 4
GRID_W = 64

SSD_HEADS = 16
SSD_HEADDIM = 64
SSD_D_INNER = SSD_HEADS * SSD_HEADDIM
SSD_GROUPS = 4
SSD_STATE = 128
SSD_CONV = 5
SSD_CHUNK = 128
SSD_XBC = SSD_D_INNER + 2 * SSD_GROUPS * SSD_STATE
HEADS_PER_GROUP = SSD_HEADS // SSD_GROUPS
GROUP_WIDTH = HEADS_PER_GROUP * SSD_HEADDIM

NA_HEADS = 16
NA_HEAD_DIM = 64
NA_WIDTH = NA_HEADS * NA_HEAD_DIM
NA_WIN_ROWS = 8
NA_WIN_COLS = 16

POOL_WINDOWS = (2, 4, 8, 16)
POOL_GROUP = 192
POOL_WIDTH = 4 * POOL_GROUP

FNET_HEADS = 4
FNET_HEAD_DIM = 192
FNET_WIDTH = FNET_HEADS * FNET_HEAD_DIM

N_BRANCH = 4
D_FF = 4 * D_MODEL
ROPE_THETA = 10000.0
DEEPNORM_ALPHA = (2 * DEPTH) ** 0.25
LN_EPS = 1e-6
RMS_EPS = 1e-5

LANES = 128
HALF = LANES // 2
DT_PAD = LANES
TOKEN_TILE = 512
NEG_BIG = -1e30
LOG2_E = 1.4426950408889634
MIB = 1024 * 1024


def _resident(shape):
    nd = len(shape)
    return pl.BlockSpec(shape, lambda *_: (0,) * nd, pipeline_mode=pl.Buffered(1))


def _layer_resident(shape, layer):
    nd = len(shape)
    return pl.BlockSpec((None,) + tuple(shape), lambda *_: (layer,) + (0,) * nd, pipeline_mode=pl.Buffered(1))


def _params(vmem_mib, n_axes=1):
    return pltpu.CompilerParams(dimension_semantics=("arbitrary",) * n_axes, vmem_limit_bytes=vmem_mib * MIB)


def _ln(x):
    mu = jnp.mean(x, axis=-1, keepdims=True)
    xc = x - mu
    var = jnp.mean(xc * xc, axis=-1, keepdims=True)
    return xc * lax.rsqrt(var + LN_EPS)


def _silu(x):
    return x * jax.nn.sigmoid(x)


def _dot(a, b):
    return jnp.dot(a, b, preferred_element_type=F32)


def _dot_nt(a, b):
    return lax.dot_general(a, b, (((1,), (1,)), ((), ())), preferred_element_type=F32)


def _ada_kernel(c_ref, w_ref, b_ref, o_ref):
    a = _silu(c_ref[...]).astype(BF16)
    o_ref[...] = _dot(a, w_ref[...].astype(BF16)) + b_ref[...]


def _ada_all_layers(c_all, ada_w, ada_b):
    rows = c_all.shape[0]
    return pl.pallas_call(
        _ada_kernel,
        out_shape=jax.ShapeDtypeStruct((DEPTH, rows, 6 * D_MODEL), F32),
        grid=(DEPTH, 6),
        in_specs=[
            _resident((rows, D_MODEL)),
            pl.BlockSpec((None, D_MODEL, D_MODEL), lambda l, j: (l, 0, j)),
            pl.BlockSpec((None, 1, D_MODEL), lambda l, j: (l, 0, j)),
        ],
        out_specs=pl.BlockSpec((None, rows, D_MODEL), lambda l, j: (l, 0, j)),
        compiler_params=_params(32, 2),
        name="ada_mod",
    )(c_all, ada_w, ada_b.reshape(DEPTH, 1, 6 * D_MODEL))


IN_SPLITS = (SSD_D_INNER, SSD_XBC, DT_PAD, 3 * NA_WIDTH, POOL_WIDTH, FNET_WIDTH)
IN_DTYPES = (F32, F32, F32, BF16, F32, F32)


def _inproj_kernel(x_ref, mod_ref, *refs):
    n = len(IN_SPLITS)
    w_refs, b_refs, o_refs = refs[:n], refs[n:2 * n], refs[2 * n:]
    h = (_ln(x_ref[...]) * (1.0 + mod_ref[1:2, :]) + mod_ref[0:1, :]).astype(BF16)
    for w_ref, b_ref, o_ref in zip(w_refs, b_refs, o_refs):
        o_ref[...] = (_dot(h, w_ref[...]) + b_ref[...]).astype(o_ref.dtype)


def _inproj(x2d, mod, tiles_per_mod, layer, weights, biases, tm):
    m = x2d.shape[0]
    if tiles_per_mod is None:
        mod_map = lambda i: (0, 0, 0)
    else:
        mod_map = lambda i: (i // tiles_per_mod, 0, 0)
    in_specs = [pl.BlockSpec((tm, D_MODEL), lambda i: (i, 0)), pl.BlockSpec((None, 8, D_MODEL), mod_map)]
    in_specs += [_layer_resident((D_MODEL, n), layer) for n in IN_SPLITS]
    in_specs += [_layer_resident((1, n), layer) for n in IN_SPLITS]
    return pl.pallas_call(
        _inproj_kernel,
        out_shape=[jax.ShapeDtypeStruct((m, n), dt) for n, dt in zip(IN_SPLITS, IN_DTYPES)],
        grid=(m // tm,),
        in_specs=in_specs,
        out_specs=[pl.BlockSpec((tm, n), lambda i: (i, 0)) for n in IN_SPLITS],
        compiler_params=_params(58),
        name="in_proj",
    )(x2d, mod, *weights, *biases)


def _row_iota(shape):
    return lax.broadcasted_iota(jnp.int32, shape, 0)


def _lane_iota(shape):
    return lax.broadcasted_iota(jnp.int32, shape, 1)


def _shifted_rows(ref, base, d, first, last):
    t = SSD_CHUNK
    if d == 0 or not ((first and d < 0) or (last and d > 0)):
        return ref[pl.ds(base + d, t), :]
    x = ref[pl.ds(base, t), :]
    row = _row_iota(x.shape)
    if d < 0:
        return jnp.where(row >= -d, pltpu.roll(x, -d, 0), 0.0)
    return jnp.where(row < t - d, pltpu.roll(x, t - d, 0), 0.0)


def _split3(x):
    hi = x.astype(BF16)
    r1 = x - hi.astype(F32)
    mid = r1.astype(BF16)
    lo = (r1 - mid.astype(F32)).astype(BF16)
    return hi, mid, lo


def _ssd_kernel(xs0_x, xs1_x, bm_x, cm_x, dt_x, z_x, xs0_c, xs1_c, bm_c, cm_c, dt_c, z_c,
                wxs, wbm, wcm, bxs, bbm, bcm, dtb_ref, alog_ref, dch_ref, nw_ref, cos_ref, sin_ref,
                ox_ref, oc_ref,
                xs_s, b_s, bt_s, c_s, dtt_s, ac_s, act_s, y_s, ea_s, sc_s, cd_s, st_s):
    t = SSD_CHUNK
    n_x, n_c = bm_x.shape[0], bm_c.shape[0]
    nc_x, nc_c = n_x // t, n_c // t
    g = pl.program_id(1)
    half = _lane_iota((t, LANES)) < HALF
    half_row = _lane_iota((1, LANES)) < HALF
    li, si = _row_iota((t, t)), _lane_iota((t, t))
    tril = jnp.where(si <= li, 1.0, 0.0).astype(BF16)
    triu = jnp.where(si >= li, 1.0, 0.0).astype(BF16)
    is_fwd_lane = (_lane_iota((t, LANES)) % (2 * HEADS_PER_GROUP)) < HEADS_PER_GROUP
    a_row = -jnp.exp(alog_ref[...])

    def conv_silu(ref, w_ref, b_ref, base, lanes, first, last):
        acc = b_ref[:, lanes] + w_ref[0:1, lanes] * _shifted_rows(ref, base, -(SSD_CONV // 2), first, last)
        for k in range(1, SSD_CONV):
            acc = acc + w_ref[k:k + 1, lanes] * _shifted_rows(ref, base, k - SSD_CONV // 2, first, last)
        return _silu(acc)

    def prep_chunk(refs, base, u0, first, last, rope):
        xs_tiles, bm_r, cm_r = refs
        dst = pl.ds(u0 + base, t)
        for p, xs_r in enumerate(xs_tiles):
            lanes = slice(p * LANES, (p + 1) * LANES)
            xs_s[dst, lanes] = conv_silu(xs_r, wxs, bxs, base, lanes, first, last)
        bm = conv_silu(bm_r, wbm, bbm, base, slice(None), first, last)
        cm = conv_silu(cm_r, wcm, bcm, base, slice(None), first, last)
        if rope:
            cos2, sin2 = cos_ref[pl.ds(base, t), :], sin_ref[pl.ds(base, t), :]
            bm = bm * cos2 + pltpu.roll(bm, HALF, 1) * sin2
            cm = cm * cos2 + pltpu.roll(cm, HALF, 1) * sin2
        b_s[dst, :] = bm.astype(BF16)
        bt_s[dst, :] = bm.T
        c_s[dst, :] = cm.astype(BF16)

    def prep_dt(dt_r, n_chunks, u0):
        def body(c, carry):
            base = pl.multiple_of(c * t, t)
            dst = pl.ds(u0 + base, t)
            raw = dt_r[pl.ds(base, t), :] + dtb_ref[...]
            dt = jnp.maximum(raw, 0.0) + jnp.log1p(jnp.exp(-jnp.abs(raw)))
            dtt_s[dst, :] = dt.T
            parts = _split3(dt * a_row)
            prefix = _dot(tril, parts[0]) + _dot(tril, parts[1]) + _dot(tril, parts[2])
            suffix = _dot(triu, parts[0]) + _dot(triu, parts[1]) + _dot(triu, parts[2])
            acum = jnp.where(is_fwd_lane, prefix, suffix) * LOG2_E
            ac_s[dst, :] = acum
            act_s[dst, :] = acum.T
            return carry

        lax.fori_loop(0, n_chunks, body, 0, unroll=min(4, n_chunks))

    @pl.when(g == 0)
    def _():
        prep_dt(dt_c, nc_c, 0)
        prep_dt(dt_x, nc_x, n_c)

    def prep_seq(refs, n_chunks, u0, rope):
        prep_chunk(refs, 0, u0, True, False, rope)

        def body(c, carry):
            prep_chunk(refs, pl.multiple_of(c * t, t), u0, False, False, rope)
            return carry

        lax.fori_loop(1, n_chunks - 1, body, 0, unroll=7 if n_chunks > 2 else 1)
        prep_chunk(refs, (n_chunks - 1) * t, u0, False, True, rope)

    prep_seq(((xs0_c, xs1_c), bm_c, cm_c), nc_c, 0, False)
    prep_seq(((xs0_x, xs1_x), bm_x, cm_x), nc_x, n_c, True)

    n_chunks = nc_c + nc_x

    def local_chunk(u, carry):
        r0 = pl.multiple_of(u * t, t)
        rows = pl.ds(r0, t)
        cb = _dot_nt(c_s[rows, :], b_s[rows, :])
        acum, bt = ac_s[rows, :], bt_s[rows, :]
        lane = _lane_iota((t, LANES))
        x_stacked = []
        for p in range(GROUP_WIDTH // LANES):
            xb = xs_s[rows, p * LANES:(p + 1) * LANES].astype(BF16)
            zero = jnp.zeros_like(xb)
            x_stacked.append(jnp.concatenate([jnp.where(half, xb, zero), jnp.where(half, zero, xb)], axis=0))
        y_tiles = [None] * (GROUP_WIDTH // LANES)
        for dirn in range(2):
            tri = (si <= li) if dirn == 0 else (si >= li)
            end_lane = t - 1 if dirn == 0 else 0
            for p in range(GROUP_WIDTH // LANES):
                lanes = slice(p * LANES, (p + 1) * LANES)
                acols, aends, m_heads, bt_heads = [], [], [], []
                for jj in range(2):
                    ln = g * (2 * HEADS_PER_GROUP) + dirn * HEADS_PER_GROUP + 2 * p + jj
                    acol = jnp.sum(jnp.where(lane == ln, acum, 0.0), axis=1, keepdims=True)
                    arow = act_s[pl.ds(r0 + ln, 1), :]
                    dtrow = dtt_s[pl.ds(r0 + ln, 1), :]
                    aend = jnp.sum(jnp.where(_lane_iota((1, t)) == end_lane, arow, 0.0), axis=1, keepdims=True)
                    decay_dt = jnp.exp2(jnp.where(tri, acol - (arow - jnp.log2(dtrow)), NEG_BIG))
                    m_heads.append((cb * decay_dt).astype(BF16))
                    bt_heads.append((bt * (dtrow * jnp.exp2(aend - arow))).astype(BF16))
                    acols.append(acol)
                    aends.append(aend)
                y = _dot(jnp.concatenate(m_heads, axis=1), x_stacked[p])
                y_tiles[p] = y if y_tiles[p] is None else y_tiles[p] + y
                sc_s[dirn, rows, lanes] = _dot(jnp.concatenate(bt_heads, axis=1), x_stacked[p])
                ea_s[dirn, rows, lanes] = jnp.exp2(jnp.where(half, acols[0], acols[1]))
                cd = jnp.exp2(jnp.where(half_row, aends[0], aends[1]))
                cd_s[dirn, pl.ds(pl.multiple_of(u * 8, 8), 8), lanes] = jnp.broadcast_to(cd, (8, LANES))
        for p, y in enumerate(y_tiles):
            y_s[rows, p * LANES:(p + 1) * LANES] = y
        return carry

    lax.fori_loop(0, n_chunks, local_chunk, 0, unroll=6)

    def carry_step(i, carry):
        u_fwd = i
        u_bwd = jnp.where(i < nc_c, nc_c - 1 - i, n_chunks + nc_c - 1 - i)
        for dirn, u in ((0, u_fwd), (1, u_bwd)):
            rows = pl.ds(pl.multiple_of(u * t, t), t)
            st = st_s[dirn]
            cd = cd_s[dirn, pl.ds(pl.multiple_of(u * 8, 8), 1), :]
            st_s[dirn] = st * cd + sc_s[dirn, rows, :]
            sc_s[dirn, rows, :] = st
        return carry

    st_s[...] = jnp.zeros_like(st_s)
    lax.fori_loop(0, n_chunks, carry_step, 0)

    def finish(z_ref, o_ref, n_chunks, u0):
        def body(c, carry):
            base = pl.multiple_of(c * t, t)
            rows = pl.ds(u0 + base, t)
            zz = z_ref[pl.ds(base, t), :]
            y = y_s[rows, :] + dch_ref[...] * xs_s[rows, :]
            for dirn in range(2):
                y = y + _dot(c_s[rows, :], sc_s[dirn, rows, :].astype(BF16)) * ea_s[dirn, rows, :]
            y = y * _silu(zz)
            ms = jnp.mean(y * y, axis=-1, keepdims=True)
            o_ref[pl.ds(base, t), :] = (y * lax.rsqrt(ms + RMS_EPS) * nw_ref[...]).astype(o_ref.dtype)
            return carry

        lax.fori_loop(0, n_chunks, body, 0, unroll=min(4, n_chunks))

    finish(z_c, oc_ref, nc_c, 0)
    finish(z_x, ox_ref, nc_x, n_c)


def _ssd(xbc_x, dt_x, z_x, xbc_c, dt_c, z_c, conv_w, conv_b, dt_bias, a_log, d_ch, norm_w, cos2, sin2, layer):
    bsz, n_x, _ = xbc_x.shape
    n_c = xbc_c.shape[1]
    n_all = n_x + n_c
    gw, st = GROUP_WIDTH, SSD_STATE
    b_blk0 = SSD_D_INNER // st
    c_blk0 = b_blk0 + SSD_GROUPS

    def seq_specs(n):
        return [
            pl.BlockSpec((None, n, LANES), lambda b, g: (b, 0, 2 * g)),
            pl.BlockSpec((None, n, LANES), lambda b, g: (b, 0, 2 * g + 1)),
            pl.BlockSpec((None, n, st), lambda b, g: (b, 0, b_blk0 + g)),
            pl.BlockSpec((None, n, st), lambda b, g: (b, 0, c_blk0 + g)),
            pl.BlockSpec((None, n, DT_PAD), lambda b, g: (b, 0, 0)),
            pl.BlockSpec((None, n, gw), lambda b, g: (b, 0, g)),
        ]

    in_specs = seq_specs(n_x) + seq_specs(n_c) + [
        pl.BlockSpec((None, 8, gw), lambda b, g: (layer, 0, g)),
        pl.BlockSpec((None, 8, st), lambda b, g: (layer, 0, b_blk0 + g)),
        pl.BlockSpec((None, 8, st), lambda b, g: (layer, 0, c_blk0 + g)),
        pl.BlockSpec((None, 1, gw), lambda b, g: (layer, 0, g)),
        pl.BlockSpec((None, 1, st), lambda b, g: (layer, 0, b_blk0 + g)),
        pl.BlockSpec((None, 1, st), lambda b, g: (layer, 0, c_blk0 + g)),
        pl.BlockSpec((None, 1, DT_PAD), lambda b, g: (layer, 0, 0)),
        pl.BlockSpec((None, 1, DT_PAD), lambda b, g: (layer, 0, 0)),
        pl.BlockSpec((None, 1, gw), lambda b, g: (layer, 0, g)),
        pl.BlockSpec((None, 1, gw), lambda b, g: (layer, 0, g)),
        _resident((n_x, st)),
        _resident((n_x, st)),
    ]
    out_shape = [jax.ShapeDtypeStruct((bsz, n_x, SSD_D_INNER), BF16),
                 jax.ShapeDtypeStruct((bsz, n_c, SSD_D_INNER), BF16)]
    out_specs = [pl.BlockSpec((None, n_x, gw), lambda b, g: (b, 0, g)),
                 pl.BlockSpec((None, n_c, gw), lambda b, g: (b, 0, g))]
    scratch = [
        pltpu.VMEM((n_all, gw), F32),
        pltpu.VMEM((n_all, st), BF16),
        pltpu.VMEM((n_all, st), F32),
        pltpu.VMEM((n_all, st), BF16),
        pltpu.VMEM((n_all, DT_PAD), F32),
        pltpu.VMEM((n_all, DT_PAD), F32),
        pltpu.VMEM((n_all, DT_PAD), F32),
        pltpu.VMEM((n_all, gw), F32),
        pltpu.VMEM((2, n_all, gw), F32),
        pltpu.VMEM((2, n_all, gw), F32),
        pltpu.VMEM((2, 8 * (n_all // SSD_CHUNK), gw), F32),
        pltpu.VMEM((2, st, gw), F32),
    ]
    return pl.pallas_call(
        _ssd_kernel,
        out_shape=out_shape,
        grid=(bsz, SSD_GROUPS),
        in_specs=in_specs,
        out_specs=out_specs,
        scratch_shapes=scratch,
        compiler_params=_params(52, 2),
        name="ssd_mixer",
    )(xbc_x, xbc_x, xbc_x, xbc_x, dt_x, z_x, xbc_c, xbc_c, xbc_c, xbc_c, dt_c, z_c,
      conv_w, conv_w, conv_w, conv_b, conv_b, conv_b, dt_bias, a_log, d_ch, norm_w, cos2, sin2)


NA_PAIR = 2
NA_DR = 2 * NA_WIN_ROWS


def _dot_tn(a, b):
    return lax.dot_general(a, b, (((0,), (0,)), ((), ())), preferred_element_type=F32)


def _pair_block_diag(q):
    lane = _lane_iota(q.shape)
    zero = jnp.zeros_like(q)
    return jnp.concatenate([jnp.where(lane < HALF, q, zero), jnp.where(lane >= HALF, q, zero)], axis=0)


def _pair_softmax_pv(scores, values, m_q):
    top = functools.reduce(jnp.maximum, [jnp.max(s, axis=0, keepdims=True) for s in scores])
    ps = [jnp.exp2(s - top) for s in scores]
    inv = 1.0 / functools.reduce(jnp.add, [jnp.sum(p, axis=0, keepdims=True) for p in ps])
    out = functools.reduce(jnp.add, [_dot_tn(p.astype(BF16), v) for p, v in zip(ps, values)])
    inv_rows = jnp.concatenate(
        [jnp.broadcast_to(inv[:, c:c + LANES], (LANES, LANES)).T for c in range(0, 2 * m_q, LANES)], axis=0)
    out = out * inv_rows
    return jnp.where(_lane_iota((m_q, LANES)) < HALF, out[:m_q], out[m_q:])


def _na_kernel(q_ref, k_ref, v_ref, qc_ref, kc_ref, vc_ref, bias_ref, ox_ref, *rest):
    oc_ref = rest[0] if len(rest) == 3 else None
    s_even, s_odd = rest[-2:]
    n = q_ref.shape[0]
    rows = n // GRID_W
    win = NA_WIN_ROWS * GRID_W
    n_c = kc_ref.shape[0]

    def window_start(r):
        rs = jnp.clip(r - NA_WIN_ROWS // 2, 0, rows - NA_WIN_ROWS)
        return rs, pl.multiple_of(rs * GRID_W, GRID_W)

    def score_row(r, s_ref):
        rs, kstart = window_start(r)
        q_bd = _pair_block_diag(q_ref[pl.ds(pl.multiple_of(r * GRID_W, GRID_W), GRID_W), :])
        dr0 = rs - r + NA_WIN_ROWS - 1
        bias = jnp.concatenate([bias_ref[dr0 + a] for a in range(NA_WIN_ROWS)], axis=0)
        s_ref[0:win, :] = _dot_nt(k_ref[pl.ds(kstart, win), :], q_bd) + bias
        s_ref[win:win + n_c, :] = _dot_nt(kc_ref[...], q_bd)

    def finish_row(r, s_ref):
        _, kstart = window_start(r)
        out = _pair_softmax_pv([s_ref[0:win, :], s_ref[win:win + n_c, :]],
                               [v_ref[pl.ds(kstart, win), :], vc_ref[...]], GRID_W)
        ox_ref[pl.ds(pl.multiple_of(r * GRID_W, GRID_W), GRID_W), :] = out.astype(ox_ref.dtype)

    score_row(0, s_even)

    def row_pair(i, carry):
        r = 2 * i
        score_row(r + 1, s_odd)
        finish_row(r, s_even)
        score_row(jnp.minimum(r + 2, rows - 1), s_even)
        finish_row(r + 1, s_odd)
        return carry

    lax.fori_loop(0, rows // 2, row_pair, 0, unroll=16)

    if oc_ref is not None:
        q_bd = _pair_block_diag(qc_ref[...])
        oc_ref[...] = _pair_softmax_pv([_dot_nt(kc_ref[...], q_bd)], [vc_ref[...]], n_c).astype(oc_ref.dtype)


def _na(qkv_x, qkv_c, bias_tab, layer, with_ctx):
    bsz, n_x, _ = qkv_x.shape
    n_c = qkv_c.shape[1]
    n_pairs = NA_HEADS // NA_PAIR
    blk = lambda n, off: pl.BlockSpec((None, n, LANES), lambda b, p: (b, 0, off + p))
    out_shape = [jax.ShapeDtypeStruct((bsz, n_x, NA_WIDTH), BF16)]
    out_specs = [blk(n_x, 0)]
    if with_ctx:
        out_shape.append(jax.ShapeDtypeStruct((bsz, n_c, NA_WIDTH), BF16))
        out_specs.append(blk(n_c, 0))
    outs = pl.pallas_call(
        _na_kernel,
        out_shape=out_shape,
        grid=(bsz, n_pairs),
        in_specs=[blk(n_x, 0), blk(n_x, n_pairs), blk(n_x, 2 * n_pairs),
                  blk(n_c, 0), blk(n_c, n_pairs), blk(n_c, 2 * n_pairs),
                  pl.BlockSpec((None, None) + tuple(bias_tab.shape[2:]), lambda b, p: (layer, p, 0, 0, 0))],
        out_specs=out_specs,
        scratch_shapes=[pltpu.VMEM((NA_WIN_ROWS * GRID_W + n_c, NA_PAIR * GRID_W), F32)] * 2,
        compiler_params=_params(32, 2),
        name="na_mixer",
    )(qkv_x, qkv_x, qkv_x, qkv_c, qkv_c, qkv_c, bias_tab)
    return outs[0], (outs[1] if with_ctx else None)


def _na_bias_kernel(v_ref, o_ref):
    shape = (GRID_W, LANES)
    kc, lane = _row_iota(shape), _lane_iota(shape)
    j = lane % GRID_W
    off = kc - jnp.clip(j - NA_WIN_COLS // 2, 0, GRID_W - NA_WIN_COLS)
    for d in range(o_ref.shape[0]):
        t0 = pltpu.roll(jnp.broadcast_to(v_ref[0, d:d + 1, :], shape), 0, 1, stride=1, stride_axis=0)
        t1 = pltpu.roll(jnp.broadcast_to(v_ref[1, d:d + 1, :], shape), HALF, 1, stride=1, stride_axis=0)
        tile = jnp.where(lane < HALF, t0, t1)
        o_ref[d] = jnp.where(off >= 0, jnp.where(off < NA_WIN_COLS, tile, NEG_BIG), NEG_BIG)


def _na_bias_table(na_rpb):
    depth, heads, n_dr, n_dc = na_rpb.shape
    m = jnp.arange(LANES)
    t = jnp.where(m < HALF, m, m - LANES)
    rows = na_rpb[:, :, :, jnp.clip(NA_WIN_COLS - 1 - t, 0, n_dc - 1)] * LOG2_E
    rows = jnp.pad(rows, ((0, 0), (0, 0), (0, NA_DR - n_dr), (0, 0)))
    return pl.pallas_call(
        _na_bias_kernel,
        out_shape=jax.ShapeDtypeStruct((depth, heads // NA_PAIR, n_dr, GRID_W, LANES), F32),
        grid=(depth, heads // NA_PAIR),
        in_specs=[pl.BlockSpec((None, NA_PAIR, NA_DR, LANES), lambda l, p: (l, p, 0, 0))],
        out_specs=pl.BlockSpec((None, None, n_dr, GRID_W, LANES), lambda l, p: (l, p, 0, 0, 0)),
        compiler_params=_params(16, 2),
        name="na_bias_table",
    )(rows)


POOL_TILES = POOL_WIDTH // LANES


def _pool_kernel(*refs):
    u_tiles, (w_ref, scale_ref, o_ref) = refs[:POOL_TILES], refs[POOL_TILES:]
    t = SSD_CHUNK
    n = o_ref.shape[0]
    n_chunks = n // t

    def tile_windows(p):
        lo_g, hi_g = (p * LANES) // POOL_GROUP, ((p + 1) * LANES - 1) // POOL_GROUP
        split = (hi_g * POOL_GROUP) - p * LANES if hi_g != lo_g else None
        return POOL_WINDOWS[lo_g], POOL_WINDOWS[hi_g], split

    def chunk(base, first, last):
        pos = base + _row_iota((t, 1))
        tiles = []
        for p, u_ref in enumerate(u_tiles):
            w_lo, w_hi, split = tile_windows(p)
            x0 = u_ref[pl.ds(base, t), :]
            sums, acc, have = {}, x0, 1
            for w in POOL_WINDOWS:
                if w > w_hi:
                    break
                for d in list(range(-(w // 2), -(have // 2))) + list(range(have - have // 2, w // 2)):
                    acc = acc + _shifted_rows(u_ref, base, d, first, last)
                have = w
                cnt = (jnp.minimum(pos + w // 2, n) - jnp.maximum(pos - w // 2, 0)).astype(F32)
                sums[w] = acc / cnt
            if split is None:
                pooled = sums[w_lo]
            else:
                pooled = jnp.where(_lane_iota((t, LANES)) < split, sums[w_lo], sums[w_hi])
            tiles.append((pooled - x0).astype(BF16))
        mapped = _dot(jnp.concatenate(tiles, axis=1), w_ref[...])
        o_ref[pl.ds(base, t), :] = (mapped * scale_ref[...]).astype(o_ref.dtype)

    chunk(0, True, False)

    def body(c, carry):
        chunk(pl.multiple_of(c * t, t), False, False)
        return carry

    lax.fori_loop(1, n_chunks - 1, body, 0)
    chunk((n_chunks - 1) * t, False, True)


def _pool(u, w_bd, scale, layer):
    bsz, n, _ = u.shape
    return pl.pallas_call(
        _pool_kernel,
        out_shape=jax.ShapeDtypeStruct((bsz, n, POOL_WIDTH), BF16),
        grid=(bsz,),
        in_specs=[pl.BlockSpec((None, n, LANES), lambda b, p=p: (b, 0, p)) for p in range(POOL_TILES)]
        + [_layer_resident((POOL_WIDTH, POOL_WIDTH), layer), _layer_resident((1, POOL_WIDTH), layer)],
        out_specs=pl.BlockSpec((None, n, POOL_WIDTH), lambda b: (b, 0, 0)),
        compiler_params=_params(40),
        name="pool_mixer",
    )(*([u] * POOL_TILES), w_bd, scale)


FNET_TILE_X = 512
FNET_TILE_C = 128
FNET_TILE_PAD = 16
FNET_CHAN_ROWS = 1024


def _fnet_kernel(u_ref, cc_ref, sc_ref, pos_ref, flip_ref, o_ref, z_s, sum_s):
    n = u_ref.shape[0]
    tq = o_ref.shape[0]
    n_lo = sum_s.shape[0]
    t = pl.program_id(1)

    @pl.when(t == 0)
    def _():
        for r in range(0, n, min(n, FNET_CHAN_ROWS)):
            ub = u_ref[r:r + min(n, FNET_CHAN_ROWS), :].astype(BF16)
            z_s[r:r + ub.shape[0], :] = _dot(ub, cc_ref[...]).astype(BF16)
            z_s[n + r:n + r + ub.shape[0], :] = _dot(ub, sc_ref[...]).astype(BF16)

    scale = 1.0 / math.sqrt(n * FNET_HEAD_DIM)

    @pl.when(t < n_lo)
    def _():
        p = _dot(pos_ref[:, 0:n], z_s[0:n, :])
        q = _dot(pos_ref[:, n:2 * n], z_s[n:2 * n, :])
        o_ref[...] = ((p - q)[0:tq] * scale).astype(o_ref.dtype)
        sum_s[t] = ((p + q) * scale).astype(sum_s.dtype)

    @pl.when(t >= n_lo)
    def _():
        src = 2 * n_lo - 1 - t
        o_ref[...] = _dot(flip_ref[...], sum_s[src, 0:tq, :]).astype(o_ref.dtype)
        o_ref[0:1, :] = sum_s[src, tq:tq + 1, :]


def _fnet(u, chan_cos, chan_sin, pos_tab, flip):
    bsz, n, _ = u.shape
    n_lo, tq_pad, _ = pos_tab.shape
    tq = tq_pad - FNET_TILE_PAD
    return pl.pallas_call(
        _fnet_kernel,
        out_shape=jax.ShapeDtypeStruct((bsz, n, FNET_WIDTH), BF16),
        grid=(bsz, 2 * n_lo),
        in_specs=[pl.BlockSpec((None, n, FNET_WIDTH), lambda b, j: (b, 0, 0)),
                  _resident((FNET_WIDTH, FNET_WIDTH)),
                  _resident((FNET_WIDTH, FNET_WIDTH)),
                  pl.BlockSpec((None, tq_pad, 2 * n), lambda b, j: (jnp.minimum(j, n_lo - 1), 0, 0)),
                  _resident((tq, tq))],
        out_specs=pl.BlockSpec((None, tq, FNET_WIDTH), lambda b, j: (b, j, 0)),
        scratch_shapes=[pltpu.VMEM((2 * n, FNET_WIDTH), BF16), pltpu.VMEM((n_lo, tq_pad, FNET_WIDTH), BF16)],
        compiler_params=_params(48, 2),
        name="fnet_mixer",
    )(u, chan_cos, chan_sin, pos_tab, flip)


def _dft_cos_sin(rows, n):
    m = jnp.arange(n, dtype=jnp.int32)
    ang = ((rows[:, None] * m[None, :]) % n).astype(F32) * (2.0 * math.pi / n)
    return jnp.cos(ang), jnp.sin(ang)


def _fnet_tables(n, tq):
    n_lo = n // (2 * tq)
    rows = (jnp.arange(n_lo, dtype=jnp.int32)[:, None] * tq + jnp.arange(tq + FNET_TILE_PAD, dtype=jnp.int32)[None, :])
    c, s = _dft_cos_sin(rows.reshape(-1), n)
    tab = jnp.concatenate([c, s], axis=1).astype(BF16).reshape(n_lo, tq + FNET_TILE_PAD, 2 * n)
    i = jnp.arange(tq, dtype=jnp.int32)[:, None]
    src = jnp.arange(tq, dtype=jnp.int32)[None, :]
    return tab, (src == tq - i).astype(BF16)


def _fnet_channel_tables():
    c, s = _dft_cos_sin(jnp.arange(FNET_HEAD_DIM, dtype=jnp.int32), FNET_HEAD_DIM)
    eye = jnp.eye(FNET_HEADS, dtype=F32)
    return jnp.kron(eye, c).astype(BF16), jnp.kron(eye, s).astype(BF16)


def _merge_kernel(x_ref, mod_ref, ssd_ref, na_ref, pool_ref, fnet_ref,
                  wg_ref, bg_ref, wssd_ref, wna_ref, wpool_ref, wfnet_ref, wmix_ref, g_ref, b_ref, o_ref):
    x = x_ref[...]
    h = (_ln(x) * (1.0 + mod_ref[1:2, :]) + mod_ref[0:1, :]).astype(BF16)
    merged = None
    branches = ((ssd_ref, wssd_ref), (na_ref, wna_ref), (pool_ref, wpool_ref), (fnet_ref, wfnet_ref))
    for i, (y_ref, w_ref) in enumerate(branches):
        cols = slice(i * D_MODEL, (i + 1) * D_MODEL)
        gate = jax.nn.sigmoid(_dot(h, wg_ref[:, cols]) + bg_ref[:, cols])
        term = gate * _dot(y_ref[...], w_ref[...])
        merged = term if merged is None else merged + term
    mix = _dot(merged.astype(BF16), wmix_ref[...])
    y = _ln(DEEPNORM_ALPHA * x + mod_ref[2:3, :] * mix)
    o_ref[...] = y * g_ref[...] + b_ref[...]


def _merge(x2d, mod, tiles_per_mod, layer, ssd, na, pool, fnet, weights, tm):
    m = x2d.shape[0]
    mod_map = (lambda i: (0, 0, 0)) if tiles_per_mod is None else (lambda i: (i // tiles_per_mod, 0, 0))
    tok = lambda n: pl.BlockSpec((tm, n), lambda i: (i, 0))
    wg, bg, wssd, wna, wpool, wfnet, wmix, ln_g, ln_b = weights
    in_specs = [tok(D_MODEL), pl.BlockSpec((None, 8, D_MODEL), mod_map),
                tok(SSD_D_INNER), tok(NA_WIDTH), tok(POOL_WIDTH), tok(FNET_WIDTH),
                _layer_resident((D_MODEL, N_BRANCH * D_MODEL), layer),
                _layer_resident((1, N_BRANCH * D_MODEL), layer),
                _layer_resident((SSD_D_INNER, D_MODEL), layer),
                _layer_resident((NA_WIDTH, D_MODEL), layer),
                _layer_resident((POOL_WIDTH, D_MODEL), layer),
                _layer_resident((FNET_WIDTH, D_MODEL), layer),
                _layer_resident((D_MODEL, D_MODEL), layer),
                _layer_resident((1, D_MODEL), layer),
                _layer_resident((1, D_MODEL), layer)]
    return pl.pallas_call(
        _merge_kernel,
        out_shape=jax.ShapeDtypeStruct((m, D_MODEL), F32),
        grid=(m // tm,),
        in_specs=in_specs,
        out_specs=tok(D_MODEL),
        compiler_params=_params(58),
        name="merge_mix",
    )(x2d, mod, ssd, na, pool, fnet, wg, bg, wssd, wna, wpool, wfnet, wmix, ln_g, ln_b)


MLP_CHUNK = 1024


def _mlp_kernel(x_ref, mod_ref, wup_ref, bup_ref, wdn_ref, bdn_ref, g_ref, b_ref, o_ref):
    x = x_ref[...]
    h = (_ln(x) * (1.0 + mod_ref[4:5, :]) + mod_ref[3:4, :]).astype(BF16)
    acc = None
    for c in range(D_FF // MLP_CHUNK):
        cols = slice(c * MLP_CHUNK, (c + 1) * MLP_CHUNK)
        a = jnp.maximum(_dot(h, wup_ref[:, cols]) + bup_ref[:, cols], 0.0)
        part = _dot((a * a).astype(BF16), wdn_ref[cols, :])
        acc = part if acc is None else acc + part
    y = _ln(DEEPNORM_ALPHA * x + mod_ref[5:6, :] * (acc + bdn_ref[...]))
    o_ref[...] = y * g_ref[...] + b_ref[...]


def _mlp(x2d, mod, tiles_per_mod, layer, weights, tm):
    m = x2d.shape[0]
    mod_map = (lambda i: (0, 0, 0)) if tiles_per_mod is None else (lambda i: (i // tiles_per_mod, 0, 0))
    tok = pl.BlockSpec((tm, D_MODEL), lambda i: (i, 0))
    wup, bup, wdn, bdn, ln_g, ln_b = weights
    in_specs = [tok, pl.BlockSpec((None, 8, D_MODEL), mod_map),
                _layer_resident((D_MODEL, D_FF), layer), _layer_resident((1, D_FF), layer),
                _layer_resident((D_FF, D_MODEL), layer), _layer_resident((1, D_MODEL), layer),
                _layer_resident((1, D_MODEL), layer), _layer_resident((1, D_MODEL), layer)]
    return pl.pallas_call(
        _mlp_kernel,
        out_shape=jax.ShapeDtypeStruct((m, D_MODEL), F32),
        grid=(m // tm,),
        in_specs=in_specs,
        out_specs=tok,
        compiler_params=_params(58),
        name="mlp",
    )(x2d, mod, wup, bup, wdn, bdn, ln_g, ln_b)


def _rope_tables(n):
    pos = jnp.arange(n)
    quarter = SSD_STATE // 4
    inv_freq = ROPE_THETA ** (-jnp.arange(quarter, dtype=F32) / quarter)
    row = (pos // GRID_W).astype(F32)[:, None] * inv_freq
    col = (pos % GRID_W).astype(F32)[:, None] * inv_freq
    ang = jnp.concatenate([row, col], -1)
    cos, sin = jnp.cos(ang), jnp.sin(ang)
    return jnp.concatenate([cos, cos], -1), jnp.concatenate([-sin, sin], -1)


def _dt_lane_layout(a):
    lead = a.shape[:-2]
    a = a.reshape(lead + (2, SSD_GROUPS, HEADS_PER_GROUP))
    a = jnp.swapaxes(a, -3, -2).reshape(lead + (2 * SSD_HEADS,))
    return jnp.pad(a, [(0, 0)] * len(lead) + [(0, DT_PAD - 2 * SSD_HEADS)])


def _block_diag(w):
    l, g, c, _ = w.shape
    eye = jnp.eye(g, dtype=w.dtype)
    return jnp.einsum("lgcd,gh->lgchd", w, eye).reshape(l, g * c, g * c)


def kernel(x, c, ctx, c_ctx, ada_w, ada_b, in_w, in_b, ssd_conv_w, ssd_conv_b, ssd_dt_bias, ssd_a_log, ssd_d,
           ssd_norm_w, ssd_out_w, na_rpb, na_out_w, pool_w, pool_scale, pool_out_w, fnet_out_w, mix_out_w,
           ln1_g, ln1_b, mlp_up_w, mlp_up_b, mlp_down_w, mlp_down_b, ln2_g, ln2_b):
    bsz, n_x, d = x.shape
    n_c = ctx.shape[1]
    depth = in_w.shape[0]

    offs = [0]
    for wdt in (SSD_D_INNER, SSD_XBC, 2 * SSD_HEADS, 3 * NA_WIDTH, POOL_WIDTH, FNET_WIDTH, N_BRANCH * D_MODEL):
        offs.append(offs[-1] + wdt)

    def in_cols(a, i):
        return a[..., offs[i]:offs[i + 1]]

    def split_in(a):
        dt_cols = in_cols(a, 2)
        dt_cols = _dt_lane_layout(dt_cols.reshape(dt_cols.shape[:-1] + (2, SSD_HEADS)))
        qkv = in_cols(a, 3)
        q_scale = jnp.concatenate([jnp.full((NA_WIDTH,), NA_HEAD_DIM ** -0.5 * LOG2_E, F32), jnp.ones((2 * NA_WIDTH,), F32)])
        return [in_cols(a, 0), in_cols(a, 1), dt_cols, qkv * q_scale, in_cols(a, 4), in_cols(a, 5)], in_cols(a, 6)

    w_parts, w_gate = split_in(in_w)
    b_parts, b_gate = split_in(in_b[:, None, :])
    w_parts = [w.astype(BF16) for w in w_parts]
    w_gate = w_gate.astype(BF16)

    conv_w = jnp.pad(ssd_conv_w, ((0, 0), (0, 8 - SSD_CONV), (0, 0)))
    conv_b = ssd_conv_b[:, None, :]
    dt_bias = _dt_lane_layout(ssd_dt_bias)[:, None, :]
    a_log = _dt_lane_layout(ssd_a_log)[:, None, :]
    d_ch = jnp.repeat(ssd_d, SSD_HEADDIM, axis=-1)[:, None, :]
    norm_w = ssd_norm_w[:, None, :]
    cos2, sin2 = _rope_tables(n_x)

    bias_tab = _na_bias_table(na_rpb)
    pool_bd = _block_diag(pool_w).astype(BF16)
    pool_sc = pool_scale[:, None, :]
    chan_cos, chan_sin = _fnet_channel_tables()
    pos_tab_x, flip_x = _fnet_tables(n_x, FNET_TILE_X)
    pos_tab_c, flip_c = _fnet_tables(n_c, FNET_TILE_C)

    merge_w = [w_gate, b_gate, ssd_out_w.astype(BF16), na_out_w.astype(BF16), pool_out_w.astype(BF16),
               fnet_out_w.astype(BF16), mix_out_w.astype(BF16), ln1_g[:, None, :], ln1_b[:, None, :]]
    mlp_w = [mlp_up_w.astype(BF16), mlp_up_b[:, None, :], mlp_down_w.astype(BF16), mlp_down_b[:, None, :],
             ln2_g[:, None, :], ln2_b[:, None, :]]

    rows = ((bsz + 1 + 7) // 8) * 8
    c_all = jnp.zeros((rows, d), F32).at[:bsz].set(c).at[bsz].set(c_ctx)
    mods = _ada_all_layers(c_all, ada_w, ada_b).reshape(depth, rows, 6, d)
    mods = jnp.pad(mods, ((0, 0), (0, 0), (0, 2), (0, 0)))

    tm = TOKEN_TILE
    tm_c = min(tm, bsz * n_c)
    assert n_x % tm == 0 and (bsz * n_c) % tm_c == 0
    tiles_x = n_x // tm
    xf = x.reshape(bsz * n_x, d)
    cf = ctx.reshape(bsz * n_c, d)
    for i in range(depth):
        need_ctx = i < depth - 1
        mod_x, mod_c = mods[i, :bsz], mods[i, bsz:bsz + 1]
        z_x, xbc_x, dt_x, qkv_x, pool_x, fnet_x = _inproj(xf, mod_x, tiles_x, i, w_parts, b_parts, tm)
        z_c, xbc_c, dt_c, qkv_c, pool_c, fnet_c = _inproj(cf, mod_c, None, i, w_parts, b_parts, tm_c)
        r3 = lambda a, n: a.reshape(bsz, n, a.shape[-1])
        ssd_x, ssd_c = _ssd(r3(xbc_x, n_x), r3(dt_x, n_x), r3(z_x, n_x), r3(xbc_c, n_c), r3(dt_c, n_c), r3(z_c, n_c),
                            conv_w, conv_b, dt_bias, a_log, d_ch, norm_w, cos2, sin2, i)
        na_x, na_c = _na(r3(qkv_x, n_x), r3(qkv_c, n_c), bias_tab, i, need_ctx)
        pm_x = _pool(r3(pool_x, n_x), pool_bd, pool_sc, i)
        fm_x = _fnet(r3(fnet_x, n_x), chan_cos, chan_sin, pos_tab_x, flip_x)
        f2 = lambda a: a.reshape(-1, a.shape[-1])
        x1 = _merge(xf, mod_x, tiles_x, i, f2(ssd_x), f2(na_x), f2(pm_x), f2(fm_x), merge_w, tm)
        xf = _mlp(x1, mod_x, tiles_x, i, mlp_w, tm)
        if need_ctx:
            pm_c = _pool(r3(pool_c, n_c), pool_bd, pool_sc, i)
            fm_c = _fnet(r3(fnet_c, n_c), chan_cos, chan_sin, pos_tab_c, flip_c)
            c1 = _merge(cf, mod_c, None, i, f2(ssd_c), f2(na_c), f2(pm_c), f2(fm_c), merge_w, tm_c)
            cf = _mlp(c1, mod_c, None, i, mlp_w, tm_c)
    return xf.reshape(bsz, n_x, d)
```
